```python
import jax, jax.numpy as jnp
from jax import lax
import numpy as np

D_MODEL = 1024
BATCH = 4
SEQ = 4096
DEPTH = 1

PLE_DIM = 256
ATTN_HEADS = 8
ATTN_HEAD_DIM = 64
ATTN_WIDTH = ATTN_HEADS * ATTN_HEAD_DIM
Q_BLOCK = 128
CONV_WIDTH = ATTN_WIDTH
CONV_K = 3
N_BRANCH = 2
PEER_HEADS = 8
PEER_NKEYS = 128
PEER_N_EXPERTS = PEER_NKEYS * PEER_NKEYS
PEER_QDIM = 256
PEER_HALF = PEER_QDIM // 2
PEER_TOPK = 16
TOKEN_CHUNK = 128
EPS = 1e-6

IN_SIZES = [ATTN_WIDTH, ATTN_WIDTH, ATTN_WIDTH, ATTN_HEADS,
            CONV_WIDTH, CONV_WIDTH, CONV_WIDTH, D_MODEL, D_MODEL]
IN_TOTAL = sum(IN_SIZES)
IN_SPLITS = np.cumsum(IN_SIZES)[:-1].tolist()

kernel_name = "fox_shortconv_peer_hybrid"


def rmsnorm(x, g):
    xf = x.astype(jnp.float32)
    y = xf * lax.rsqrt(jnp.mean(xf * xf, axis=-1, keepdims=True) + EPS) * g.astype(jnp.float32)
    return y.astype(x.dtype)


def forgetting_attention(q, k, v, f_logit):
    B, S, H, Dh = q.shape
    nb = S // Q_BLOCK
    log_f = jax.nn.log_sigmoid(f_logit.astype(jnp.float32))
    c = jnp.cumsum(log_f, axis=1).transpose(0, 2, 1)
    qb = q.reshape(B, nb, Q_BLOCK, H, Dh).transpose(1, 0, 3, 2, 4)
    cqb = c.reshape(B, H, nb, Q_BLOCK).transpose(2, 0, 1, 3)
    kt = k.transpose(0, 2, 1, 3)
    vt = v.transpose(0, 2, 1, 3)
    k_pos = jnp.arange(S)
    scale = ATTN_HEAD_DIM ** -0.5

    def block(args):
        qi, cqi, i = args
        logits = (jnp.einsum('bhqd,bhkd->bhqk', qi, kt).astype(jnp.float32) * scale
                  + cqi[..., :, None] - c[:, :, None, :])
        q_pos = i * Q_BLOCK + jnp.arange(Q_BLOCK)
        mask = k_pos[None, :] <= q_pos[:, None]
        logits = jnp.where(mask, logits, -jnp.inf)
        w = jax.nn.softmax(logits, axis=-1)
        return jnp.einsum('bhqk,bhkd->bhqd', w.astype(vt.dtype), vt)

    out = lax.map(block, (qb, cqb, jnp.arange(nb)))
    return out.transpose(1, 0, 3, 2, 4).reshape(B, S, H * Dh)


def causal_depthwise_conv(z, w):
    C = z.shape[-1]
    return lax.conv_general_dilated(
        z, w[:, None, :].astype(z.dtype), window_strides=(1,),
        padding=[(CONV_K - 1, 0)], dimension_numbers=('NWC', 'WIO', 'NWC'),
        feature_group_count=C)


def peer(h, w_q, k1, k2, u, v):
    B, S, D = h.shape
    q = jnp.einsum('bsd,dhe->bshe', h, w_q)
    s1 = jnp.einsum('bshe,hne->bshn', q[..., :PEER_HALF], k1).astype(jnp.float32)
    s2 = jnp.einsum('bshe,hne->bshn', q[..., PEER_HALF:], k2).astype(jnp.float32)
    v1, i1 = lax.top_k(s1, PEER_TOPK)
    v2, i2 = lax.top_k(s2, PEER_TOPK)
    cand_s = (v1[..., :, None] + v2[..., None, :]).reshape(B, S, PEER_HEADS, PEER_TOPK * PEER_TOPK)
    cand_i = (i1[..., :, None] * PEER_NKEYS + i2[..., None, :]).reshape(B, S, PEER_HEADS, PEER_TOPK * PEER_TOPK)
    top_s, top_j = lax.top_k(cand_s, PEER_TOPK)
    ids = jnp.take_along_axis(cand_i, top_j, axis=-1)
    gates = jax.nn.softmax(top_s, axis=-1)
    T = B * S
    nc = T // TOKEN_CHUNK
    hf = h.reshape(nc, TOKEN_CHUNK, D)
    idf = ids.reshape(nc, TOKEN_CHUNK, PEER_HEADS, PEER_TOPK)
    gf = gates.reshape(nc, TOKEN_CHUNK, PEER_HEADS, PEER_TOPK)

    def chunk(args):
        hc, ic, gc = args
        u_sel = u[ic]
        act = jax.nn.gelu(jnp.einsum('td,thkd->thk', hc, u_sel), approximate=False)
        wgt = gc.astype(hc.dtype) * act
        return jnp.einsum('thk,thkd->td', wgt, v[ic])

    return lax.map(chunk, (hf, idf, gf)).reshape(B, S, D)


def setup_inputs(seed: int = 0) -> dict:
    key = jax.random.key(seed)
    ks = jax.random.split(key, 20)
    D = D_MODEL
    f32 = jnp.float32
    nrm = lambda k, shape, s: jax.random.normal(k, shape, f32) * s
    return {
        "x": nrm(ks[0], (BATCH, SEQ, D), 1.0),
        "p": nrm(ks[1], (DEPTH, BATCH, SEQ, PLE_DIM), 1.0),
        "g_mix": 1.0 + nrm(ks[2], (DEPTH, D), 0.02),
        "w_in": nrm(ks[3], (DEPTH, D, IN_TOTAL), D ** -0.5),
        "b_f": jax.random.uniform(ks[4], (DEPTH, ATTN_HEADS), f32, 1.0, 6.0),
        "conv_w": nrm(ks[5], (DEPTH, CONV_K, CONV_WIDTH), CONV_K ** -0.5),
        "w_branch": nrm(ks[6], (DEPTH, N_BRANCH, ATTN_WIDTH, D), ATTN_WIDTH ** -0.5),
        "w_out": nrm(ks[7], (DEPTH, D, D), D ** -0.5),
        "g_ffn": 1.0 + nrm(ks[8], (DEPTH, D), 0.02),
        "w_peer_q": nrm(ks[9], (DEPTH, D, PEER_HEADS, PEER_QDIM), D ** -0.5),
        "peer_k1": nrm(ks[10], (DEPTH, PEER_HEADS, PEER_NKEYS, PEER_HALF), PEER_HALF ** -0.5),
        "peer_k2": nrm(ks[11], (DEPTH, PEER_HEADS, PEER_NKEYS, PEER_HALF), PEER_HALF ** -0.5),
        "peer_u": nrm(ks[12], (DEPTH, PEER_N_EXPERTS, D), D ** -0.5),
        "peer_v": nrm(ks[13], (DEPTH, PEER_N_EXPERTS, D), PEER_HEADS ** -0.5),
        "g_ple": 1.0 + nrm(ks[14], (DEPTH, D), 0.02),
        "w_ple_gate": nrm(ks[15], (DEPTH, D, D), D ** -0.5),
        "w_ple_proj": nrm(ks[16], (DEPTH, PLE_DIM, D), PLE_DIM ** -0.5),
        "g_final": 1.0 + nrm(ks[17], (D,), 0.02),
    }


def reference(x, p, g_mix, w_in, b_f, conv_w, w_branch, w_out, g_ffn, w_peer_q,
              peer_k1, peer_k2, peer_u, peer_v, g_ple, w_ple_gate, w_ple_proj, g_final):
    B, S, D = x.shape
    for i in range(DEPTH):
        h = rmsnorm(x, g_mix[i])
        proj = h @ w_in[i]
        q, k, v, f_logit, gb, gc_in, xc, g_attn, g_conv = jnp.split(proj, IN_SPLITS, axis=-1)
        attn = forgetting_attention(
            q.reshape(B, S, ATTN_HEADS, ATTN_HEAD_DIM),
            k.reshape(B, S, ATTN_HEADS, ATTN_HEAD_DIM),
            v.reshape(B, S, ATTN_HEADS, ATTN_HEAD_DIM),
            f_logit + b_f[i])
        conv = gb * causal_depthwise_conv(gc_in * xc, conv_w[i])
        branches = jnp.stack([attn, conv], axis=0)
        y_br = jnp.einsum('nbsc,ncd->nbsd', branches, w_branch[i])
        gates = jax.nn.sigmoid(jnp.stack([g_attn, g_conv], axis=0))
        merged = jnp.sum(gates * y_br, axis=0)
        x = x + merged @ w_out[i]
        x = x + peer(rmsnorm(x, g_ffn[i]), w_peer_q[i], peer_k1[i], peer_k2[i], peer_u[i], peer_v[i])
        ple_gate = jax.nn.sigmoid(rmsnorm(x, g_ple[i]) @ w_ple_gate[i])
        x = x + ple_gate * (p[i] @ w_ple_proj[i])
    return rmsnorm(x, g_final)
```

```python
import functools
import math

import jax
import jax.numpy as jnp
from jax import lax
from jax.experimental import pallas as pl
from jax.experimental.pallas import tpu as pltpu

EPS = 1e-6
PEER_TOPK = 16
LANES = 128
NEG_BIG = -1e30
VMEM_LIMIT = 48 * 1024 * 1024

F32 = jnp.float32
BF16 = jnp.bfloat16


def _cparams(*sem):
    return pltpu.CompilerParams(dimension_semantics=sem, vmem_limit_bytes=VMEM_LIMIT)


def _rms(x, g):
    return x * lax.rsqrt(jnp.mean(x * x, axis=-1, keepdims=True) + EPS) * g


def _log_sigmoid(x):
    return jnp.minimum(x, 0.0) - jnp.log1p(jnp.exp(-jnp.abs(x)))


def _split3(x):
    hi = x.astype(BF16)
    r = x - hi.astype(F32)
    mid = r.astype(BF16)
    lo = (r - mid.astype(F32)).astype(BF16)
    return hi, mid, lo


def _in_proj_kernel(tiles_per_seq, n_heads, scale,
                    x_ref, g_ref, wq_ref, wk_ref, wv_ref, wf_ref, bf_ref, wgb_ref, wgc_ref,
                    wxc_ref, wga_ref, wgv_ref, tri_ref,
                    q_ref, k_ref, v_ref, c_ref, gb_ref, z_ref, ga_ref, gv_ref, carry_ref):
    i = pl.program_id(0)
    h = _rms(x_ref[...], g_ref[...]).astype(BF16)

    def mm(w_ref):
        return jnp.dot(h, w_ref[...], preferred_element_type=F32)

    q_ref[...] = (mm(wq_ref) * scale).astype(BF16)
    k_ref[...] = mm(wk_ref).astype(BF16)
    v_ref[...] = mm(wv_ref).astype(BF16)
    gb_ref[...] = mm(wgb_ref)
    z_ref[...] = mm(wgc_ref) * mm(wxc_ref)
    ga_ref[...] = mm(wga_ref)
    gv_ref[...] = mm(wgv_ref)

    logf = _log_sigmoid(mm(wf_ref) + bf_ref[...])
    lane = lax.broadcasted_iota(jnp.int32, logf.shape, 1)
    logf = jnp.where(lane < n_heads, logf, 0.0)
    tri = tri_ref[...]
    hi, mid, lo = _split3(logf)
    cs = (jnp.dot(tri, hi, preferred_element_type=F32)
          + jnp.dot(tri, mid, preferred_element_type=F32)
          + jnp.dot(tri, lo, preferred_element_type=F32))

    @pl.when(i % tiles_per_seq == 0)
    def _():
        carry_ref[...] = jnp.zeros_like(carry_ref)

    c_ref[...] = cs + carry_ref[...]
    tm = c_ref.shape[0]
    carry_ref[...] = c_ref[tm - 1:tm, :]


def _in_proj(x2, g_mix, w_in, b_f, sizes, seq, tm):
    T, D = x2.shape
    aw, n_heads, cw = sizes
    offs = [0, aw, 2 * aw, 3 * aw, 3 * aw + n_heads, 3 * aw + n_heads + cw,
            3 * aw + n_heads + 2 * cw, 3 * aw + n_heads + 3 * cw,
            3 * aw + n_heads + 3 * cw + D, 3 * aw + n_heads + 3 * cw + 2 * D]
    seg = [w_in[:, offs[n]:offs[n + 1]] for n in range(9)]
    wq, wk, wv, wf, wgb, wgc, wxc, wga, wgv = seg
    wf = jnp.pad(wf, ((0, 0), (0, LANES - n_heads)))
    bf = jnp.pad(b_f.reshape(1, n_heads), ((0, 0), (0, LANES - n_heads)))
    ws = [w.astype(BF16) for w in (wq, wk, wv, wf)] + [bf] + \
         [w.astype(BF16) for w in (wgb, wgc, wxc, wga, wgv)]
    tri = (lax.broadcasted_iota(jnp.int32, (tm, tm), 0)
           >= lax.broadcasted_iota(jnp.int32, (tm, tm), 1)).astype(BF16)
    scale = (aw // n_heads) ** -0.5

    def full(a):
        return pl.BlockSpec(a.shape, lambda i: (0, 0))

    def tok(width):
        return pl.BlockSpec((tm, width), lambda i: (i, 0))

    outs = [(aw, BF16), (aw, BF16), (aw, BF16), (LANES, F32), (cw, F32), (cw, F32),
            (D, F32), (D, F32)]
    return pl.pallas_call(
        functools.partial(_in_proj_kernel, seq // tm, n_heads, scale),
        grid=(T // tm,),
        in_specs=[tok(D), full(g_mix)] + [full(w) for w in ws] + [full(tri)],
        out_specs=[tok(w) for w, _ in outs],
        out_shape=[jax.ShapeDtypeStruct((T, w), dt) for w, dt in outs],
        scratch_shapes=[pltpu.VMEM((1, LANES), F32)],
        compiler_params=_cparams("arbitrary"),
        name="in_proj",
    )(x2, g_mix, *ws, tri)


def _fox_attn_kernel(tq, dh, q_ref, k_ref, v_ref, c_ref, o_ref, m_ref, l_ref, acc_ref):
    qi = pl.program_id(2)
    qp = q_ref[0]
    lane = lax.broadcasted_iota(jnp.int32, qp.shape, 1)
    zero = jnp.zeros_like(qp)
    qs = (jnp.where(lane < dh, qp, zero), jnp.where(lane >= dh, qp, zero))

    m_ref[...] = jnp.full(m_ref.shape, NEG_BIG, F32)
    l_ref[...] = jnp.zeros(l_ref.shape, F32)
    acc_ref[...] = jnp.zeros(acc_ref.shape, F32)

    d0 = pl.multiple_of(qi * tq, tq)
    refs = [c_ref[0, 0, hh:hh + 1, pl.ds(d0, tq)][:, 0:1] for hh in range(2)]

    def step(j, diag):
        k0 = pl.multiple_of(j * tq, tq)
        kb = k_ref[0, pl.ds(k0, tq), :]
        vb = v_ref[0, pl.ds(k0, tq), :]
        for hh in range(2):
            s = lax.dot_general(qs[hh], kb, (((1,), (1,)), ((), ())),
                                preferred_element_type=F32)
            s = s + (refs[hh] - c_ref[0, 0, hh:hh + 1, pl.ds(k0, tq)])
            if diag:
                row = lax.broadcasted_iota(jnp.int32, s.shape, 0)
                col = lax.broadcasted_iota(jnp.int32, s.shape, 1)
                s = jnp.where(col <= row, s, NEG_BIG)
            m_prev = m_ref[hh]
            m_new = jnp.maximum(m_prev, jnp.max(s, axis=1, keepdims=True))
            alpha = jnp.exp(m_prev - m_new)
            p = jnp.exp(s - m_new)
            l_ref[hh] = alpha * l_ref[hh] + jnp.sum(p, axis=1, keepdims=True)
            acc_ref[hh] = alpha * acc_ref[hh] + jnp.dot(p.astype(BF16), vb,
                                                        preferred_element_type=F32)
            m_ref[hh] = m_new

    def body(j, carry):
        step(j, False)
        return carry

    lax.fori_loop(0, qi, body, 0)
    step(qi, True)

    out = jnp.where(lane < dh, acc_ref[0] / l_ref[0], acc_ref[1] / l_ref[1])
    o_ref[0] = out.astype(o_ref.dtype)


def _fox_attn(q, k, v, c_t, n_heads, tq):
    B, S, AW = q.shape
    dh = AW // n_heads
    assert 2 * dh == LANES and n_heads % 2 == 0
    kv_spec = pl.BlockSpec((1, S, LANES), lambda b, p, i: (b, 0, p))
    return pl.pallas_call(
        functools.partial(_fox_attn_kernel, tq, dh),
        grid=(B, n_heads // 2, S // tq),
        in_specs=[pl.BlockSpec((1, tq, LANES), lambda b, p, i: (b, i, p)), kv_spec, kv_spec,
                  pl.BlockSpec((1, 1, 2, S), lambda b, p, i: (b, p, 0, 0))],
        out_specs=pl.BlockSpec((1, tq, LANES), lambda b, p, i: (b, i, p)),
        out_shape=jax.ShapeDtypeStruct((B, S, AW), BF16),
        scratch_shapes=[pltpu.VMEM((2, tq, 1), F32), pltpu.VMEM((2, tq, 1), F32),
                        pltpu.VMEM((2, tq, LANES), F32)],
        compiler_params=_cparams("arbitrary", "arbitrary", "arbitrary"),
        name="fox_attn",
    )(q, k, v, c_t)


def _merge_kernel(tiles_per_seq, halo,
                  x_ref, a_ref, gb_ref, z_ref, zprev_ref, cw_ref, ga_ref, gv_ref,
                  wua_ref, wuc_ref, wo_ref, o_ref):
    i = pl.program_id(0)
    z = z_ref[...]
    tm = z.shape[0]
    prev = jnp.where(i % tiles_per_seq == 0, 0.0, zprev_ref[...])
    zext = jnp.concatenate([prev, z], axis=0)
    cw = cw_ref[...]
    kk = cw.shape[0]
    conv = z * cw[kk - 1:kk, :]
    for d in range(1, kk):
        conv = conv + pltpu.roll(zext, d, axis=0)[halo:halo + tm] * cw[kk - 1 - d:kk - d, :]
    c = (gb_ref[...] * conv).astype(BF16)
    ya = jnp.dot(a_ref[...], wua_ref[...], preferred_element_type=F32)
    yc = jnp.dot(c, wuc_ref[...], preferred_element_type=F32)
    merged = jax.nn.sigmoid(ga_ref[...]) * ya + jax.nn.sigmoid(gv_ref[...]) * yc
    o_ref[...] = x_ref[...] + jnp.dot(merged.astype(BF16), wo_ref[...],
                                      preferred_element_type=F32)


def _merge(x2, attn, gb, z, conv_w, ga, gv, w_branch, w_out, seq, tm):
    T, D = x2.shape
    aw = attn.shape[1]
    cw = z.shape[1]
    halo = 8
    assert conv_w.shape[0] - 1 <= halo
    wua = w_branch[0].astype(BF16)
    wuc = w_branch[1].astype(BF16)
    wo = w_out.astype(BF16)

    def full(a):
        return pl.BlockSpec(a.shape, lambda i: (0, 0))

    def tok(width):
        return pl.BlockSpec((tm, width), lambda i: (i, 0))

    hb = tm // halo
    return pl.pallas_call(
        functools.partial(_merge_kernel, seq // tm, halo),
        grid=(T // tm,),
        in_specs=[tok(D), tok(aw), tok(cw), tok(cw),
                  pl.BlockSpec((halo, cw), lambda i: (jnp.maximum(i * hb - 1, 0), 0)),
                  full(conv_w), tok(D), tok(D), full(wua), full(wuc), full(wo)],
        out_specs=tok(D),
        out_shape=jax.ShapeDtypeStruct((T, D), F32),
        compiler_params=_cparams("arbitrary"),
        name="merge",
    )(x2, attn, gb, z, z, conv_w, ga, gv, wua, wuc, wo)


def _top_rows(x, k):
    rows = []
    for _ in range(k):
        m = jnp.max(x, axis=0, keepdims=True)
        rows.append(m)
        x = jnp.where(x == m, -jnp.inf, x)
    return rows


def _stack_rows(rows, t):
    n = len(rows)
    ridx = lax.broadcasted_iota(jnp.int32, (n, t), 0)
    out = jnp.zeros((n, t), F32)
    for r, row in enumerate(rows):
        out = jnp.where(ridx == r, row, out)
    return out


def _peer_prep_kernel(n_heads, half, topk,
                      x_ref, g_ref, wqt_ref, k1_ref, k2_ref,
                      ht_ref, s1_ref, s2_ref, e2_ref, c_ref, tau_ref):
    h2 = _rms(x_ref[...], g_ref[...])
    ht = h2.T.astype(BF16)
    ht_ref[...] = ht
    qt = jnp.dot(wqt_ref[...], ht, preferred_element_type=F32)
    tp = ht.shape[1]
    for h in range(n_heads):
        q1 = qt[h * 2 * half:h * 2 * half + half].astype(BF16)
        q2 = qt[h * 2 * half + half:(h + 1) * 2 * half].astype(BF16)
        s1 = jnp.dot(k1_ref[h], q1, preferred_element_type=F32)
        s2 = jnp.dot(k2_ref[h], q2, preferred_element_type=F32)
        v1 = _top_rows(s1, topk)
        v2 = _top_rows(s2, topk)
        v2a = _stack_rows(v2, tp)
        parts = []
        for a in range(topk):
            nb = topk // (a + 1)
            nb = min(topk, -(-nb // 8) * 8)
            parts.append(v1[a] + v2a[:nb])
        cand = jnp.concatenate(parts, axis=0)
        top = _top_rows(cand, topk)
        zsum = jnp.zeros_like(top[0])
        for r in range(topk):
            zsum = zsum + jnp.exp(top[r] - top[0])
        s1_ref[h] = s1
        s2_ref[h] = s2
        e2_ref[h] = jnp.exp(s2 - v2[0])
        c_ref[h] = jnp.exp(s1 - v1[0]) / zsum
        tau_ref[h] = top[topk - 1]


def _peer_prep(x1, g_ffn, w_peer_q, k1, k2, tp):
    T, D = x1.shape
    _, PH, QD = w_peer_q.shape
    NK, HALF = k1.shape[1], k1.shape[2]
    assert NK >= PEER_TOPK and PEER_TOPK % 8 == 0
    wqt = w_peer_q.reshape(D, PH * QD).T.astype(BF16)
    k1b = k1.astype(BF16)
    k2b = k2.astype(BF16)
    big = pl.BlockSpec((PH, NK, tp), lambda i: (0, 0, i))
    big_shape = jax.ShapeDtypeStruct((PH, NK, T), F32)
    return pl.pallas_call(
        functools.partial(_peer_prep_kernel, PH, HALF, PEER_TOPK),
        grid=(T // tp,),
        in_specs=[pl.BlockSpec((tp, D), lambda i: (i, 0)),
                  pl.BlockSpec(g_ffn.shape, lambda i: (0, 0)),
                  pl.BlockSpec(wqt.shape, lambda i: (0, 0)),
                  pl.BlockSpec(k1b.shape, lambda i: (0, 0, 0)),
                  pl.BlockSpec(k2b.shape, lambda i: (0, 0, 0))],
        out_specs=[pl.BlockSpec((D, tp), lambda i: (0, i)), big, big, big, big,
                   pl.BlockSpec((PH, 1, tp), lambda i: (0, 0, i))],
        out_shape=[jax.ShapeDtypeStruct((D, T), BF16), big_shape, big_shape, big_shape,
                   big_shape, jax.ShapeDtypeStruct((PH, 1, T), F32)],
        compiler_params=_cparams("arbitrary"),
        name="peer_prep",
    )(x1, g_ffn, wqt, k1b, k2b)


def _gelu(x):
    return 0.5 * x * (1.0 + lax.erf(x * (1.0 / math.sqrt(2.0))))


def _peer_main_kernel(n_heads, nkeys, rows,
                      x_ref, ht_ref, u_ref, v_ref, s1_ref, c_ref, s2_ref, e2_ref, tau_ref,
                      o_ref, acc_ref, w_ref):
    j = pl.program_id(1)

    @pl.when(j == 0)
    def _():
        acc_ref[...] = jnp.zeros_like(acc_ref)

    ht = ht_ref[...]
    for r in range(rows):
        act = jnp.dot(u_ref[r * nkeys:(r + 1) * nkeys, :], ht, preferred_element_type=F32)
        g = jnp.zeros_like(act)
        for h in range(n_heads):
            score = s1_ref[h, r:r + 1, :] + s2_ref[h]
            g = g + jnp.where(score >= tau_ref[h], e2_ref[h] * c_ref[h, r:r + 1, :], 0.0)
        w = g * _gelu(act)
        w_ref[:, r * nkeys:(r + 1) * nkeys] = w.T.astype(BF16)
    acc_ref[...] += jnp.dot(w_ref[...], v_ref[...], preferred_element_type=F32)

    @pl.when(j == pl.num_programs(1) - 1)
    def _():
        o_ref[...] = x_ref[...] + acc_ref[...]


def _peer_main(x1, ht, u, v, s1, c, s2, e2, tau, tm):
    T, D = x1.shape
    PH, NK, _ = s1.shape
    NE = u.shape[0]
    rows = min(8, NK)
    eb = rows * NK
    ub = u.astype(BF16)
    vb = v.astype(BF16)
    tile = pl.BlockSpec((PH, NK, tm), lambda i, j: (0, 0, i))
    rowblk = pl.BlockSpec((PH, rows, tm), lambda i, j: (0, j, i))
    return pl.pallas_call(
        functools.partial(_peer_main_kernel, PH, NK, rows),
        grid=(T // tm, NE // eb),
        in_specs=[pl.BlockSpec((tm, D), lambda i, j: (i, 0)),
                  pl.BlockSpec((D, tm), lambda i, j: (0, i)),
                  pl.BlockSpec((eb, D), lambda i, j: (j, 0)),
                  pl.BlockSpec((eb, D), lambda i, j: (j, 0)),
                  rowblk, rowblk, tile, tile,
                  pl.BlockSpec((PH, 1, tm), lambda i, j: (0, 0, i))],
        out_specs=pl.BlockSpec((tm, D), lambda i, j: (i, 0)),
        out_shape=jax.ShapeDtypeStruct((T, D), F32),
        scratch_shapes=[pltpu.VMEM((tm, D), F32), pltpu.VMEM((tm, eb), BF16)],
        compiler_params=_cparams("arbitrary", "arbitrary"),
        name="peer_main",
    )(x1, ht, ub, vb, s1, c, s2, e2, tau)


def _ple_out_kernel(final_norm, x_ref, p_ref, gp_ref, wg_ref, wp_ref, gf_ref, o_ref):
    x = x_ref[...]
    h = _rms(x, gp_ref[...]).astype(BF16)
    gate = jax.nn.sigmoid(jnp.dot(h, wg_ref[...], preferred_element_type=F32))
    proj = jnp.dot(p_ref[...].astype(BF16), wp_ref[...], preferred_element_type=F32)
    y = x + gate * proj
    o_ref[...] = _rms(y, gf_ref[...]) if final_norm else y


def _ple_out(x2, p2, g_ple, w_gate, w_proj, g_final, final_norm, tm):
    T, D = x2.shape
    wg = w_gate.astype(BF16)
    wp = w_proj.astype(BF16)

    def full(a):
        return pl.BlockSpec(a.shape, lambda i: (0, 0))

    return pl.pallas_call(
        functools.partial(_ple_out_kernel, final_norm),
        grid=(T // tm,),
        in_specs=[pl.BlockSpec((tm, D), lambda i: (i, 0)),
                  pl.BlockSpec((tm, p2.shape[1]), lambda i: (i, 0)),
                  full(g_ple), full(wg), full(wp), full(g_final)],
        out_specs=pl.BlockSpec((tm, D), lambda i: (i, 0)),
        out_shape=jax.ShapeDtypeStruct((T, D), F32),
        compiler_params=_cparams("arbitrary"),
        name="ple_out",
    )(x2, p2, g_ple, wg, wp, g_final)


def _tile(n, want):
    t = min(n, want)
    assert n % t == 0
    return t


def kernel(x, p, g_mix, w_in, b_f, conv_w, w_branch, w_out, g_ffn, w_peer_q, peer_k1,
           peer_k2, peer_u, peer_v, g_ple, w_ple_gate, w_ple_proj, g_final):
    B, S, D = x.shape
    T = B * S
    depth = w_in.shape[0]
    n_heads = b_f.shape[1]
    aw = w_branch.shape[2]
    cw = conv_w.shape[2]
    tm = _tile(S, 256)
    xt = x.reshape(T, D)
    for i in range(depth):
        q, k, v, c, gb, z, ga, gv = _in_proj(
            xt, g_mix[i].reshape(1, D), w_in[i], b_f[i], (aw, n_heads, cw), S, tm)
        c_t = c[:, :n_heads].reshape(B, S, n_heads // 2, 2).transpose(0, 2, 3, 1)
        attn = _fox_attn(q.reshape(B, S, aw), k.reshape(B, S, aw), v.reshape(B, S, aw),
                         c_t, n_heads, _tile(S, 256))
        x1 = _merge(xt, attn.reshape(T, aw), gb, z, conv_w[i], ga, gv, w_branch[i], w_out[i],
                    S, tm)
        ht, s1, s2, e2, cc, tau = _peer_prep(x1, g_ffn[i].reshape(1, D), w_peer_q[i],
                                             peer_k1[i], peer_k2[i], _tile(T, 256))
        x2 = _peer_main(x1, ht, peer_u[i], peer_v[i], s1, cc, s2, e2, tau, _tile(T, 512))
        xt = _ple_out(x2, p[i].reshape(T, p.shape[-1]), g_ple[i].reshape(1, D),
                      w_ple_gate[i], w_ple_proj[i], g_final.reshape(1, D), i == depth - 1, tm)
    return xt.reshape(B, S, D)
```

```python
import functools
import math

import jax
import jax.numpy as jnp
import numpy as np
from jax import lax
from jax.experimental import pallas as pl
from jax.experimental.pallas import tpu as pltpu

EPS = 1e-6
PEER_TOPK = 16
LANES = 128
NEG_BIG = -1e30
VMEM_LIMIT = 48 * 1024 * 1024

F32 = jnp.float32
BF16 = jnp.bfloat16


def _cparams(*sem):
    return pltpu.CompilerParams(dimension_semantics=sem, vmem_limit_bytes=VMEM_LIMIT)


def _rms(x, g):
    return x * lax.rsqrt(jnp.mean(x * x, axis=-1, keepdims=True) + EPS) * g


def _log_sigmoid(x):
    return jnp.minimum(x, 0.0) - jnp.log1p(jnp.exp(-jnp.abs(x)))


def _split3(x):
    hi = x.astype(BF16)
    r = x - hi.astype(F32)
    mid = r.astype(BF16)
    lo = (r - mid.astype(F32)).astype(BF16)
    return hi, mid, lo


def _in_proj_kernel(tiles_per_seq, n_heads, scale,
                    x_ref, g_ref, wq_ref, wk_ref, wv_ref, wf_ref, bf_ref, wgb_ref, wgc_ref,
                    wxc_ref, wga_ref, wgv_ref, tri_ref, place_ref,
                    q_ref, k_ref, v_ref, gb_ref, z_ref, ga_ref, gv_ref, carry_ref):
    i = pl.program_id(0)
    h = _rms(x_ref[...], g_ref[...]).astype(BF16)

    def mm(w_ref):
        return jnp.dot(h, w_ref[...], preferred_element_type=F32)

    q_ref[...] = (mm(wq_ref) * scale).astype(BF16)
    v_ref[...] = mm(wv_ref).astype(BF16)
    gb_ref[...] = mm(wgb_ref)
    z_ref[...] = mm(wgc_ref) * mm(wxc_ref)
    ga_ref[...] = mm(wga_ref)
    gv_ref[...] = mm(wgv_ref)

    logf = _log_sigmoid(mm(wf_ref) + bf_ref[...])
    lane = lax.broadcasted_iota(jnp.int32, logf.shape, 1)
    logf = jnp.where(lane < n_heads, logf, 0.0)
    tri = tri_ref[...]
    hi, mid, lo = _split3(logf)
    cs = (jnp.dot(tri, hi, preferred_element_type=F32)
          + jnp.dot(tri, mid, preferred_element_type=F32)
          + jnp.dot(tri, lo, preferred_element_type=F32))

    @pl.when(i % tiles_per_seq == 0)
    def _():
        carry_ref[...] = jnp.zeros_like(carry_ref)

    c = cs + carry_ref[...]
    tm = c.shape[0]
    carry_ref[...] = c[tm - 1:tm, :]
    pieces = jnp.concatenate(_split3(-c), axis=1)
    k_ref[...] = (mm(wk_ref) + jnp.dot(pieces, place_ref[...],
                                       preferred_element_type=F32)).astype(BF16)


def _key_layout(n_heads, dh):
    def feat(h, d):
        return (h // 2) * 2 * LANES + (h % 2) * dh + d

    def bias(h, s):
        return (h // 2) * 2 * LANES + LANES + (h % 2) * 3 + s

    return feat, bias


def _in_proj(x2, g_mix, w_in, b_f, sizes, seq, tm):
    T, D = x2.shape
    aw, n_heads, cw = sizes
    offs = [0, aw, 2 * aw, 3 * aw, 3 * aw + n_heads, 3 * aw + n_heads + cw,
            3 * aw + n_heads + 2 * cw, 3 * aw + n_heads + 3 * cw,
            3 * aw + n_heads + 3 * cw + D, 3 * aw + n_heads + 3 * cw + 2 * D]
    seg = [w_in[:, offs[n]:offs[n + 1]] for n in range(9)]
    wq, wk, wv, wf, wgb, wgc, wxc, wga, wgv = seg
    wf = jnp.pad(wf, ((0, 0), (0, LANES - n_heads)))
    bf = jnp.pad(b_f.reshape(1, n_heads), ((0, 0), (0, LANES - n_heads)))
    dh = aw // n_heads
    kw = n_heads * LANES
    wk = jnp.pad(wk.reshape(D, n_heads // 2, 2 * dh),
                 ((0, 0), (0, 0), (0, 2 * LANES - 2 * dh))).reshape(D, kw)
    _, bias_lane = _key_layout(n_heads, dh)
    place = np.zeros((3 * LANES, kw), np.float32)
    for hd in range(n_heads):
        for s in range(3):
            place[s * LANES + hd, bias_lane(hd, s)] = 1.0
    place = jnp.asarray(place, BF16)
    ws = [w.astype(BF16) for w in (wq, wk, wv, wf)] + [bf] + \
         [w.astype(BF16) for w in (wgb, wgc, wxc, wga, wgv)]
    tri = (lax.broadcasted_iota(jnp.int32, (tm, tm), 0)
           >= lax.broadcasted_iota(jnp.int32, (tm, tm), 1)).astype(BF16)
    scale = dh ** -0.5

    def full(a):
        return pl.BlockSpec(a.shape, lambda i: (0, 0))

    def tok(width):
        return pl.BlockSpec((tm, width), lambda i: (i, 0))

    outs = [(aw, BF16), (kw, BF16), (aw, BF16), (cw, F32), (cw, F32), (D, F32), (D, F32)]
    return pl.pallas_call(
        functools.partial(_in_proj_kernel, seq // tm, n_heads, scale),
        grid=(T // tm,),
        in_specs=[tok(D), full(g_mix)] + [full(w) for w in ws] + [full(tri), full(place)],
        out_specs=[tok(w) for w, _ in outs],
        out_shape=[jax.ShapeDtypeStruct((T, w), dt) for w, dt in outs],
        scratch_shapes=[pltpu.VMEM((1, LANES), F32)],
        compiler_params=_cparams("arbitrary"),
        name="in_proj",
    )(x2, g_mix, *ws, tri, place)


def _fox_attn_kernel(tq, dh, q_ref, k_ref, v_ref, o_ref, m_ref, acc_ref):
    qi = pl.program_id(2)
    qp = q_ref[0]
    lane = lax.broadcasted_iota(jnp.int32, qp.shape, 1)
    zero = jnp.zeros_like(qp)
    one = jnp.ones_like(qp)
    q_aug = []
    for hh in range(2):
        feat = jnp.where(lane < dh, qp, zero) if hh == 0 else jnp.where(lane >= dh, qp, zero)
        pick = jnp.where(lane < 3 * hh, 0.0, jnp.where(lane < 3 * hh + 3, 1.0, 0.0)).astype(BF16)
        q_aug.append(jnp.concatenate([feat, pick], axis=1))

    m_ref[...] = jnp.full(m_ref.shape, NEG_BIG, F32)
    acc_ref[...] = jnp.zeros(acc_ref.shape, F32)

    def step(j, diag):
        k0 = pl.multiple_of(j * tq, tq)
        kb = k_ref[0, pl.ds(k0, tq), :]
        vb = v_ref[0, pl.ds(k0, tq), :]
        v_aug = (jnp.where(lane < dh, vb, one), jnp.where(lane >= dh, vb, one))
        for hh in range(2):
            s = lax.dot_general(q_aug[hh], kb, (((1,), (1,)), ((), ())),
                                preferred_element_type=F32)
            if diag:
                row = lax.broadcasted_iota(jnp.int32, s.shape, 0)
                col = lax.broadcasted_iota(jnp.int32, s.shape, 1)
                s = jnp.where(col <= row, s, NEG_BIG)
            m_prev = m_ref[hh]
            m_new = jnp.maximum(m_prev, jnp.max(s, axis=1, keepdims=True))
            alpha = jnp.exp(m_prev - m_new)
            p = jnp.exp(s - pltpu.repeat(m_new, tq // LANES, axis=1))
            acc_ref[hh] = alpha * acc_ref[hh] + jnp.dot(p.astype(BF16), v_aug[hh],
                                                        preferred_element_type=F32)
            m_ref[hh] = m_new

    def body(j, carry):
        step(j, False)
        return carry

    lax.fori_loop(0, qi, body, 0)
    step(qi, True)

    a0 = acc_ref[0]
    a1 = acc_ref[1]
    out = jnp.where(lane < dh, a0 / pltpu.roll(a0, dh, axis=1), a1 / pltpu.roll(a1, dh, axis=1))
    o_ref[0] = out.astype(o_ref.dtype)


def _fox_attn(q, k, v, n_heads, tq):
    B, S, AW = q.shape
    dh = AW // n_heads
    assert 2 * dh == LANES and n_heads % 2 == 0 and tq % LANES == 0
    return pl.pallas_call(
        functools.partial(_fox_attn_kernel, tq, dh),
        grid=(B, n_heads // 2, S // tq),
        in_specs=[pl.BlockSpec((1, tq, LANES), lambda b, p, i: (b, i, p)),
                  pl.BlockSpec((1, S, 2 * LANES), lambda b, p, i: (b, 0, p)),
                  pl.BlockSpec((1, S, LANES), lambda b, p, i: (b, 0, p))],
        out_specs=pl.BlockSpec((1, tq, LANES), lambda b, p, i: (b, i, p)),
        out_shape=jax.ShapeDtypeStruct((B, S, AW), BF16),
        scratch_shapes=[pltpu.VMEM((2, tq, LANES), F32), pltpu.VMEM((2, tq, LANES), F32)],
        compiler_params=_cparams("arbitrary", "arbitrary", "arbitrary"),
        name="fox_attn",
    )(q, k, v)


def _merge_kernel(tiles_per_seq, halo,
                  x_ref, a_ref, gb_ref, z_ref, zprev_ref, cw_ref, ga_ref, gv_ref,
                  wua_ref, wuc_ref, wo_ref, o_ref):
    i = pl.program_id(0)
    z = z_ref[...]
    tm = z.shape[0]
    prev = jnp.where(i % tiles_per_seq == 0, 0.0, zprev_ref[...])
    zext = jnp.concatenate([prev, z], axis=0)
    cw = cw_ref[...]
    kk = cw.shape[0]
    conv = z * cw[kk - 1:kk, :]
    for d in range(1, kk):
        conv = conv + pltpu.roll(zext, d, axis=0)[halo:halo + tm] * cw[kk - 1 - d:kk - d, :]
    c = (gb_ref[...] * conv).astype(BF16)
    ya = jnp.dot(a_ref[...], wua_ref[...], preferred_element_type=F32)
    yc = jnp.dot(c, wuc_ref[...], preferred_element_type=F32)
    merged = jax.nn.sigmoid(ga_ref[...]) * ya + jax.nn.sigmoid(gv_ref[...]) * yc
    o_ref[...] = x_ref[...] + jnp.dot(merged.astype(BF16), wo_ref[...],
                                      preferred_element_type=F32)


def _merge(x2, attn, gb, z, conv_w, ga, gv, w_branch, w_out, seq, tm):
    T, D = x2.shape
    aw = attn.shape[1]
    cw = z.shape[1]
    halo = 8
    assert conv_w.shape[0] - 1 <= halo
    wua = w_branch[0].astype(BF16)
    wuc = w_branch[1].astype(BF16)
    wo = w_out.astype(BF16)

    def full(a):
        return pl.BlockSpec(a.shape, lambda i: (0, 0))

    def tok(width):
        return pl.BlockSpec((tm, width), lambda i: (i, 0))

    hb = tm // halo
    return pl.pallas_call(
        functools.partial(_merge_kernel, seq // tm, halo),
        grid=(T // tm,),
        in_specs=[tok(D), tok(aw), tok(cw), tok(cw),
                  pl.BlockSpec((halo, cw), lambda i: (jnp.maximum(i * hb - 1, 0), 0)),
                  full(conv_w), tok(D), tok(D), full(wua), full(wuc), full(wo)],
        out_specs=tok(D),
        out_shape=jax.ShapeDtypeStruct((T, D), F32),
        compiler_params=_cparams("arbitrary"),
        name="merge",
    )(x2, attn, gb, z, z, conv_w, ga, gv, wua, wuc, wo)


def _top_rows(x, k, want_rank=False):
    rows = []
    rank = jnp.full(x.shape, float(k), F32)
    for r in range(k):
        m = jnp.max(x, axis=0, keepdims=True)
        rows.append(m)
        hit = x == m
        if want_rank:
            rank = jnp.where(hit, float(r), rank)
        x = jnp.where(hit, -jnp.inf, x)
    return (rows, rank) if want_rank else rows


def _stack_rows(rows, t):
    n = len(rows)
    ridx = lax.broadcasted_iota(jnp.int32, (n, t), 0)
    out = jnp.zeros((n, t), F32)
    for r, row in enumerate(rows):
        out = jnp.where(ridx == r, row, out)
    return out


def _peer_prep_kernel(n_heads, half, topk,
                      x_ref, g_ref, wqt_ref, k1_ref, k2_ref,
                      ht_ref, rk_ref, e2_ref, c_ref, n_ref):
    h2 = _rms(x_ref[...], g_ref[...])
    ht = h2.T.astype(BF16)
    ht_ref[...] = ht
    qt = jnp.dot(wqt_ref[...], ht, preferred_element_type=F32)
    tp = ht.shape[1]
    for h in range(n_heads):
        q1 = qt[h * 2 * half:h * 2 * half + half].astype(BF16)
        q2 = qt[h * 2 * half + half:(h + 1) * 2 * half].astype(BF16)
        s1 = jnp.dot(k1_ref[h], q1, preferred_element_type=F32)
        s2 = jnp.dot(k2_ref[h], q2, preferred_element_type=F32)
        v1, rank1 = _top_rows(s1, topk, True)
        v2, rank2 = _top_rows(s2, topk, True)
        v2a = _stack_rows(v2, tp)
        parts = []
        for a in range(topk):
            nb = topk // (a + 1)
            nb = min(topk, -(-nb // 8) * 8)
            parts.append(v1[a] + v2a[:nb])
        top = _top_rows(jnp.concatenate(parts, axis=0), topk)
        tau = top[topk - 1]
        zsum = jnp.zeros_like(tau)
        for r in range(topk):
            zsum = zsum + jnp.exp(top[r] - top[0])
        n = jnp.zeros_like(s1)
        for a in range(topk):
            n_a = jnp.sum(jnp.where(parts[a] >= tau, 1.0, 0.0), axis=0, keepdims=True)
            n = jnp.where(rank1 == float(a), n_a, n)
        rk_ref[h] = rank2.astype(rk_ref.dtype)
        e2_ref[h] = jnp.exp(s2 - v2[0]).astype(e2_ref.dtype)
        c_ref[h] = jnp.exp(s1 - v1[0]) / zsum
        n_ref[h] = n


def _peer_prep(x1, g_ffn, w_peer_q, k1, k2, tp):
    T, D = x1.shape
    _, PH, QD = w_peer_q.shape
    NK, HALF = k1.shape[1], k1.shape[2]
    assert NK >= PEER_TOPK and PEER_TOPK % 8 == 0
    wqt = w_peer_q.reshape(D, PH * QD).T.astype(BF16)
    k1b = k1.astype(BF16)
    k2b = k2.astype(BF16)
    big = pl.BlockSpec((PH, NK, tp), lambda i: (0, 0, i))
    return pl.pallas_call(
        functools.partial(_peer_prep_kernel, PH, HALF, PEER_TOPK),
        grid=(T // tp,),
        in_specs=[pl.BlockSpec((tp, D), lambda i: (i, 0)),
                  pl.BlockSpec(g_ffn.shape, lambda i: (0, 0)),
                  pl.BlockSpec(wqt.shape, lambda i: (0, 0)),
                  pl.BlockSpec(k1b.shape, lambda i: (0, 0, 0)),
                  pl.BlockSpec(k2b.shape, lambda i: (0, 0, 0))],
        out_specs=[pl.BlockSpec((D, tp), lambda i: (0, i)), big, big, big, big],
        out_shape=[jax.ShapeDtypeStruct((D, T), BF16),
                   jax.ShapeDtypeStruct((PH, NK, T), BF16),
                   jax.ShapeDtypeStruct((PH, NK, T), BF16),
                   jax.ShapeDtypeStruct((PH, NK, T), F32),
                   jax.ShapeDtypeStruct((PH, NK, T), F32)],
        compiler_params=_cparams("arbitrary"),
        name="peer_prep",
    )(x1, g_ffn, wqt, k1b, k2b)


def _gelu(x):
    return 0.5 * x * (1.0 + lax.erf(x * (1.0 / math.sqrt(2.0))))


def _peer_main_kernel(n_heads, nkeys, rows,
                      x_ref, ht_ref, u_ref, vt_ref, rk_ref, e2_ref, c_ref, n_ref,
                      o_ref, acc_ref, w_ref):
    j = pl.program_id(1)

    @pl.when(j == 0)
    def _():
        acc_ref[...] = jnp.zeros_like(acc_ref)

    ht = ht_ref[...]
    tm = ht.shape[1]
    for r in range(rows):
        act = jnp.dot(u_ref[r * nkeys:(r + 1) * nkeys, :], ht, preferred_element_type=F32)
        g = jnp.zeros((nkeys, tm), BF16)
        for h in range(n_heads):
            n_b = jnp.broadcast_to(n_ref[h, r:r + 1, :].astype(BF16), (nkeys, tm))
            c_b = jnp.broadcast_to(c_ref[h, r:r + 1, :].astype(BF16), (nkeys, tm))
            g = g + jnp.where(rk_ref[h] < n_b, e2_ref[h] * c_b, jnp.zeros_like(g))
        w_ref[r * nkeys:(r + 1) * nkeys, :] = g * _gelu(act).astype(BF16)
    acc_ref[...] += jnp.dot(vt_ref[...], w_ref[...], preferred_element_type=F32)

    @pl.when(j == pl.num_programs(1) - 1)
    def _():
        o_ref[...] = x_ref[...] + acc_ref[...].T


def _peer_main(x1, ht, u, v, rk, e2, c, n, tm):
    T, D = x1.shape
    PH, NK, _ = rk.shape
    NE = u.shape[0]
    rows = min(8, NK)
    eb = rows * NK
    ub = u.astype(BF16)
    vt = v.T.astype(BF16)
    tile = pl.BlockSpec((PH, NK, tm), lambda i, j: (0, 0, i))
    rowblk = pl.BlockSpec((PH, rows, tm), lambda i, j: (0, j, i))
    return pl.pallas_call(
        functools.partial(_peer_main_kernel, PH, NK, rows),
        grid=(T // tm, NE // eb),
        in_specs=[pl.BlockSpec((tm, D), lambda i, j: (i, 0)),
                  pl.BlockSpec((D, tm), lambda i, j: (0, i)),
                  pl.BlockSpec((eb, D), lambda i, j: (j, 0)),
                  pl.BlockSpec((D, eb), lambda i, j: (0, j)),
                  tile, tile, rowblk, rowblk],
        out_specs=pl.BlockSpec((tm, D), lambda i, j: (i, 0)),
        out_shape=jax.ShapeDtypeStruct((T, D), F32),
        scratch_shapes=[pltpu.VMEM((D, tm), F32), pltpu.VMEM((eb, tm), BF16)],
        compiler_params=_cparams("arbitrary", "arbitrary"),
        name="peer_main",
    )(x1, ht, ub, vt, rk, e2, c, n)


def _ple_out_kernel(final_norm, x_ref, p_ref, gp_ref, wg_ref, wp_ref, gf_ref, o_ref):
    x = x_ref[...]
    h = _rms(x, gp_ref[...]).astype(BF16)
    gate = jax.nn.sigmoid(jnp.dot(h, wg_ref[...], preferred_element_type=F32))
    proj = jnp.dot(p_ref[...].astype(BF16), wp_ref[...], preferred_element_type=F32)
    y = x + gate * proj
    o_ref[...] = _rms(y, gf_ref[...]) if final_norm else y


def _ple_out(x2, p2, g_ple, w_gate, w_proj, g_final, final_norm, tm):
    T, D = x2.shape
    wg = w_gate.astype(BF16)
    wp = w_proj.astype(BF16)

    def full(a):
        return pl.BlockSpec(a.shape, lambda i: (0, 0))

    return pl.pallas_call(
        functools.partial(_ple_out_kernel, final_norm),
        grid=(T // tm,),
        in_specs=[pl.BlockSpec((tm, D), lambda i: (i, 0)),
                  pl.BlockSpec((tm, p2.shape[1]), lambda i: (i, 0)),
                  full(g_ple), full(wg), full(wp), full(g_final)],
        out_specs=pl.BlockSpec((tm, D), lambda i: (i, 0)),
        out_shape=jax.ShapeDtypeStruct((T, D), F32),
        compiler_params=_cparams("arbitrary"),
        name="ple_out",
    )(x2, p2, g_ple, wg, wp, g_final)


def _tile(n, want):
    t = min(n, want)
    assert n % t == 0
    return t


def kernel(x, p, g_mix, w_in, b_f, conv_w, w_branch, w_out, g_ffn, w_peer_q, peer_k1,
           peer_k2, peer_u, peer_v, g_ple, w_ple_gate, w_ple_proj, g_final):
    B, S, D = x.shape
    T = B * S
    depth = w_in.shape[0]
    n_heads = b_f.shape[1]
    aw = w_branch.shape[2]
    cw = conv_w.shape[2]
    tm = _tile(S, 256)
    xt = x.reshape(T, D)
    for i in range(depth):
        q, k, v, gb, z, ga, gv = _in_proj(
            xt, g_mix[i].reshape(1, D), w_in[i], b_f[i], (aw, n_heads, cw), S, tm)
        attn = _fox_attn(q.reshape(B, S, aw), k.reshape(B, S, -1), v.reshape(B, S, aw),
                         n_heads, _tile(S, 512))
        x1 = _merge(xt, attn.reshape(T, aw), gb, z, conv_w[i], ga, gv, w_branch[i], w_out[i],
                    S, tm)
        ht, rk, e2, cc, nn = _peer_prep(x1, g_ffn[i].reshape(1, D), w_peer_q[i],
                                        peer_k1[i], peer_k2[i], _tile(T, 256))
        x2 = _peer_main(x1, ht, peer_u[i], peer_v[i], rk, e2, cc, nn, _tile(T, 512))
        xt = _ple_out(x2, p[i].reshape(T, p.shape[-1]), g_ple[i].reshape(1, D),
                      w_ple_gate[i], w_ple_proj[i], g_final.reshape(1, D), i == depth - 1, tm)
    return xt.reshape(B, S, D)
```

```python
import functools
import math

import jax
import jax.numpy as jnp
import numpy as np
from jax import lax
from jax.experimental import pallas as pl
from jax.experimental.pallas import tpu as pltpu

EPS = 1e-6
PEER_TOPK = 16
LANES = 128
NEG_BIG = -1e30
REMOVED = 2.0 ** 100
TOK_CHUNK = 256
VMEM_LIMIT = 56 * 1024 * 1024

F32 = jnp.float32
BF16 = jnp.bfloat16


def _cparams(*sem, flags=None):
    return pltpu.CompilerParams(dimension_semantics=sem, vmem_limit_bytes=VMEM_LIMIT,
                                flags=flags)


def _rms(x, g):
    return x * lax.rsqrt(jnp.mean(x * x, axis=-1, keepdims=True) + EPS) * g


def _log_sigmoid(x):
    return jnp.minimum(x, 0.0) - jnp.log1p(jnp.exp(-jnp.abs(x)))


def _split3(x):
    hi = x.astype(BF16)
    r = x - hi.astype(F32)
    mid = r.astype(BF16)
    lo = (r - mid.astype(F32)).astype(BF16)
    return hi, mid, lo


def _in_proj_kernel(tiles_per_seq, n_heads, scale,
                    x_ref, g_ref, wq_ref, wk_ref, wv_ref, wf_ref, bf_ref, wgb_ref, wgc_ref,
                    wxc_ref, wga_ref, wgv_ref, tri_ref, place_ref,
                    q_ref, k_ref, v_ref, gb_ref, z_ref, ga_ref, gv_ref, carry_ref):
    i = pl.program_id(0)
    h = _rms(x_ref[...], g_ref[...]).astype(BF16)

    def mm(w_ref):
        return jnp.dot(h, w_ref[...], preferred_element_type=F32)

    q_ref[...] = (mm(wq_ref) * scale).astype(BF16)
    v_ref[...] = mm(wv_ref).astype(BF16)
    gb_ref[...] = mm(wgb_ref)
    z_ref[...] = mm(wgc_ref) * mm(wxc_ref)
    ga_ref[...] = mm(wga_ref)
    gv_ref[...] = mm(wgv_ref)

    logf = _log_sigmoid(mm(wf_ref) + bf_ref[...])
    lane = lax.broadcasted_iota(jnp.int32, logf.shape, 1)
    logf = jnp.where(lane < n_heads, logf, 0.0)
    tri = tri_ref[...]
    hi, mid, lo = _split3(logf)
    cs = (jnp.dot(tri, hi, preferred_element_type=F32)
          + jnp.dot(tri, mid, preferred_element_type=F32)
          + jnp.dot(tri, lo, preferred_element_type=F32))

    @pl.when(i % tiles_per_seq == 0)
    def _():
        carry_ref[...] = jnp.zeros_like(carry_ref)

    c = cs + carry_ref[...]
    tm = c.shape[0]
    carry_ref[...] = c[tm - 1:tm, :]
    pieces = jnp.concatenate(_split3(-c), axis=1)
    k_ref[...] = (mm(wk_ref) + jnp.dot(pieces, place_ref[...],
                                       preferred_element_type=F32)).astype(BF16)


def _key_layout(n_heads, dh):
    def feat(h, d):
        return (h // 2) * 2 * LANES + (h % 2) * dh + d

    def bias(h, s):
        return (h // 2) * 2 * LANES + LANES + (h % 2) * 3 + s

    return feat, bias


def _in_proj(x2, g_mix, w_in, b_f, sizes, seq, tm):
    T, D = x2.shape
    aw, n_heads, cw = sizes
    offs = [0, aw, 2 * aw, 3 * aw, 3 * aw + n_heads, 3 * aw + n_heads + cw,
            3 * aw + n_heads + 2 * cw, 3 * aw + n_heads + 3 * cw,
            3 * aw + n_heads + 3 * cw + D, 3 * aw + n_heads + 3 * cw + 2 * D]
    seg = [w_in[:, offs[n]:offs[n + 1]] for n in range(9)]
    wq, wk, wv, wf, wgb, wgc, wxc, wga, wgv = seg
    wf = jnp.pad(wf, ((0, 0), (0, LANES - n_heads)))
    bf = jnp.pad(b_f.reshape(1, n_heads), ((0, 0), (0, LANES - n_heads)))
    dh = aw // n_heads
    kw = n_heads * LANES
    wk = jnp.pad(wk.reshape(D, n_heads // 2, 2 * dh),
                 ((0, 0), (0, 0), (0, 2 * LANES - 2 * dh))).reshape(D, kw)
    _, bias_lane = _key_layout(n_heads, dh)
    place = np.zeros((3 * LANES, kw), np.float32)
    for hd in range(n_heads):
        for s in range(3):
            place[s * LANES + hd, bias_lane(hd, s)] = 1.0
    place = jnp.asarray(place, BF16)
    ws = [w.astype(BF16) for w in (wq, wk, wv, wf)] + [bf] + \
         [w.astype(BF16) for w in (wgb, wgc, wxc, wga, wgv)]
    tri = (lax.broadcasted_iota(jnp.int32, (tm, tm), 0)
           >= lax.broadcasted_iota(jnp.int32, (tm, tm), 1)).astype(BF16)
    scale = dh ** -0.5

    def full(a):
        return pl.BlockSpec(a.shape, lambda i: (0, 0))

    def tok(width):
        return pl.BlockSpec((tm, width), lambda i: (i, 0))

    outs = [(aw, BF16), (kw, BF16), (aw, BF16), (cw, F32), (cw, F32), (D, F32), (D, F32)]
    return pl.pallas_call(
        functools.partial(_in_proj_kernel, seq // tm, n_heads, scale),
        grid=(T // tm,),
        in_specs=[tok(D), full(g_mix)] + [full(w) for w in ws] + [full(tri), full(place)],
        out_specs=[tok(w) for w, _ in outs],
        out_shape=[jax.ShapeDtypeStruct((T, w), dt) for w, dt in outs],
        scratch_shapes=[pltpu.VMEM((1, LANES), F32)],
        compiler_params=_cparams("arbitrary"),
        name="in_proj",
    )(x2, g_mix, *ws, tri, place)


def _fox_attn_kernel(tq, tk, dh, q_ref, k_ref, v_ref, o_ref, m_ref, acc_ref):
    qi = pl.program_id(2)
    qp = q_ref[0]
    lane = lax.broadcasted_iota(jnp.int32, qp.shape, 1)
    zero = jnp.zeros_like(qp)
    q_aug = []
    for hh in range(2):
        feat = jnp.where(lane < dh, qp, zero) if hh == 0 else jnp.where(lane >= dh, qp, zero)
        pick = jnp.where(lane < 3 * hh, 0.0, jnp.where(lane < 3 * hh + 3, 1.0, 0.0)).astype(BF16)
        q_aug.append(jnp.concatenate([feat, pick], axis=1))

    m_ref[...] = jnp.full(m_ref.shape, NEG_BIG, F32)
    acc_ref[...] = jnp.zeros(acc_ref.shape, F32)

    def step(k0, width, diag):
        kb = k_ref[0, pl.ds(k0, width), :]
        vb = v_ref[0, pl.ds(k0, width), :]
        vlane = lax.broadcasted_iota(jnp.int32, vb.shape, 1)
        vone = jnp.ones_like(vb)
        v_aug = (jnp.where(vlane < dh, vb, vone), jnp.where(vlane >= dh, vb, vone))
        for hh in range(2):
            s = lax.dot_general(q_aug[hh], kb, (((1,), (1,)), ((), ())),
                                preferred_element_type=F32)
            if diag:
                row = lax.broadcasted_iota(jnp.int32, s.shape, 0)
                col = lax.broadcasted_iota(jnp.int32, s.shape, 1)
                s = jnp.where(col <= row, s, NEG_BIG)
            m_prev = m_ref[hh]
            m_new = jnp.maximum(m_prev, jnp.max(s, axis=1, keepdims=True))
            alpha = jnp.exp(m_prev - m_new)
            p = jnp.exp(s - jnp.tile(m_new, (1, width // LANES)))
            acc_ref[hh] = alpha * acc_ref[hh] + jnp.dot(p.astype(BF16), v_aug[hh],
                                                        preferred_element_type=F32)
            m_ref[hh] = m_new

    def body(j, carry):
        step(pl.multiple_of(j * tk, tk), tk, False)
        return carry

    q0 = qi * tq
    n_wide = q0 // tk
    lax.fori_loop(0, n_wide, body, 0)

    def narrow(j, carry):
        step(pl.multiple_of(n_wide * tk + j * tq, tq), tq, False)
        return carry

    lax.fori_loop(0, (q0 - n_wide * tk) // tq, narrow, 0)
    step(pl.multiple_of(q0, tq), tq, True)

    a0 = acc_ref[0]
    a1 = acc_ref[1]
    out = jnp.where(lane < dh, a0 / pltpu.roll(a0, dh, axis=1), a1 / pltpu.roll(a1, dh, axis=1))
    o_ref[0] = out.astype(o_ref.dtype)


def _fox_attn(q, k, v, n_heads, tq, tk):
    B, S, AW = q.shape
    dh = AW // n_heads
    assert 2 * dh == LANES and n_heads % 2 == 0 and tk % tq == 0 and tk % LANES == 0
    return pl.pallas_call(
        functools.partial(_fox_attn_kernel, tq, tk, dh),
        grid=(B, n_heads // 2, S // tq),
        in_specs=[pl.BlockSpec((1, tq, LANES), lambda b, p, i: (b, i, p)),
                  pl.BlockSpec((1, S, 2 * LANES), lambda b, p, i: (b, 0, p)),
                  pl.BlockSpec((1, S, LANES), lambda b, p, i: (b, 0, p))],
        out_specs=pl.BlockSpec((1, tq, LANES), lambda b, p, i: (b, i, p)),
        out_shape=jax.ShapeDtypeStruct((B, S, AW), BF16),
        scratch_shapes=[pltpu.VMEM((2, tq, LANES), F32), pltpu.VMEM((2, tq, LANES), F32)],
        compiler_params=_cparams("arbitrary", "arbitrary", "arbitrary"),
        name="fox_attn",
    )(q, k, v)


def _merge_kernel(tiles_per_seq, halo,
                  x_ref, a_ref, gb_ref, z_ref, zprev_ref, cw_ref, ga_ref, gv_ref,
                  wua_ref, wuc_ref, wo_ref, o_ref):
    i = pl.program_id(0)
    z = z_ref[...]
    tm = z.shape[0]
    prev = jnp.where(i % tiles_per_seq == 0, 0.0, zprev_ref[...])
    zext = jnp.concatenate([prev, z], axis=0)
    cw = cw_ref[...]
    kk = cw.shape[0]
    conv = z * cw[kk - 1:kk, :]
    for d in range(1, kk):
        conv = conv + pltpu.roll(zext, d, axis=0)[halo:halo + tm] * cw[kk - 1 - d:kk - d, :]
    c = (gb_ref[...] * conv).astype(BF16)
    ya = jnp.dot(a_ref[...], wua_ref[...], preferred_element_type=F32)
    yc = jnp.dot(c, wuc_ref[...], preferred_element_type=F32)
    merged = jax.nn.sigmoid(ga_ref[...]) * ya + jax.nn.sigmoid(gv_ref[...]) * yc
    o_ref[...] = x_ref[...] + jnp.dot(merged.astype(BF16), wo_ref[...],
                                      preferred_element_type=F32)


def _merge(x2, attn, gb, z, conv_w, ga, gv, w_branch, w_out, seq, tm):
    T, D = x2.shape
    aw = attn.shape[1]
    cw = z.shape[1]
    halo = 8
    assert conv_w.shape[0] - 1 <= halo
    wua = w_branch[0].astype(BF16)
    wuc = w_branch[1].astype(BF16)
    wo = w_out.astype(BF16)

    def full(a):
        return pl.BlockSpec(a.shape, lambda i: (0, 0))

    def tok(width):
        return pl.BlockSpec((tm, width), lambda i: (i, 0))

    hb = tm // halo
    return pl.pallas_call(
        functools.partial(_merge_kernel, seq // tm, halo),
        grid=(T // tm,),
        in_specs=[tok(D), tok(aw), tok(cw), tok(cw),
                  pl.BlockSpec((halo, cw), lambda i: (jnp.maximum(i * hb - 1, 0), 0)),
                  full(conv_w), tok(D), tok(D), full(wua), full(wuc), full(wo)],
        out_specs=tok(D),
        out_shape=jax.ShapeDtypeStruct((T, D), F32),
        compiler_params=_cparams("arbitrary"),
        name="merge",
    )(x2, attn, gb, z, z, conv_w, ga, gv, wua, wuc, wo)


def _top_rows(x, k, want_rank=False):
    rows = []
    for r in range(k):
        m = jnp.max(x, axis=0, keepdims=True)
        rows.append(m)
        x = jnp.where(x == m, -REMOVED * (1.0 + r * 2.0 ** -23), x)
    if not want_rank:
        return rows
    rank = jnp.where(x <= -REMOVED, (x * (-1.0 / REMOVED) - 1.0) * 2.0 ** 23, float(k))
    return rows, rank


def _stack_rows(rows, t):
    n = len(rows)
    ridx = lax.broadcasted_iota(jnp.int32, (n, t), 0)
    out = jnp.zeros((n, t), F32)
    for r, row in enumerate(rows):
        out = jnp.where(ridx == r, row, out)
    return out


def _peer_prep_kernel(n_heads, half, topk,
                      x_ref, g_ref, wqt_ref, k1_ref, k2_ref,
                      ht_ref, rk_ref, e2_ref, c_ref, n_ref):
    h2 = _rms(x_ref[...], g_ref[...])
    ht = h2.T.astype(BF16)
    ht_ref[...] = ht
    qt = jnp.dot(wqt_ref[...], ht, preferred_element_type=F32)
    tp = ht.shape[1]
    for h in range(n_heads):
        q1 = qt[h * 2 * half:h * 2 * half + half].astype(BF16)
        q2 = qt[h * 2 * half + half:(h + 1) * 2 * half].astype(BF16)
        s1 = jnp.dot(k1_ref[h], q1, preferred_element_type=F32)
        s2 = jnp.dot(k2_ref[h], q2, preferred_element_type=F32)
        v1, rank1 = _top_rows(s1, topk, True)
        v2, rank2 = _top_rows(s2, topk, True)
        v2a = _stack_rows(v2, tp)
        parts = []
        for a in range(topk):
            nb = topk // (a + 1)
            nb = min(topk, -(-nb // 8) * 8)
            parts.append(v1[a] + v2a[:nb])
        top = _top_rows(jnp.concatenate(parts, axis=0), topk)
        tau = top[topk - 1]
        zsum = jnp.zeros_like(tau)
        for r in range(topk):
            zsum = zsum + jnp.exp(top[r] - top[0])
        n = jnp.zeros_like(s1)
        for a in range(topk):
            n_a = jnp.sum(jnp.where(parts[a] >= tau, 1.0, 0.0), axis=0, keepdims=True)
            n = jnp.where(rank1 == float(a), n_a, n)
        rank2 = rank2.astype(rk_ref.dtype)
        e2 = jnp.exp(s2 - v2[0]).astype(e2_ref.dtype)
        for t in range(tp // TOK_CHUNK):
            rk_ref[h, t] = rank2[:, t * TOK_CHUNK:(t + 1) * TOK_CHUNK]
            e2_ref[h, t] = e2[:, t * TOK_CHUNK:(t + 1) * TOK_CHUNK]
        c_ref[h] = jnp.exp(s1 - v1[0]) / zsum
        n_ref[h] = n


def _peer_prep(x1, g_ffn, w_peer_q, k1, k2, tp):
    T, D = x1.shape
    _, PH, QD = w_peer_q.shape
    NK, HALF = k1.shape[1], k1.shape[2]
    assert NK >= PEER_TOPK and PEER_TOPK % 8 == 0
    wqt = w_peer_q.reshape(D, PH * QD).T.astype(BF16)
    k1b = k1.astype(BF16)
    k2b = k2.astype(BF16)
    big = pl.BlockSpec((PH, NK, tp), lambda i: (0, 0, i))
    tiled = pl.BlockSpec((PH, tp // TOK_CHUNK, NK, TOK_CHUNK), lambda i: (0, i, 0, 0))
    return pl.pallas_call(
        functools.partial(_peer_prep_kernel, PH, HALF, PEER_TOPK),
        grid=(T // tp,),
        in_specs=[pl.BlockSpec((tp, D), lambda i: (i, 0)),
                  pl.BlockSpec(g_ffn.shape, lambda i: (0, 0)),
                  pl.BlockSpec(wqt.shape, lambda i: (0, 0)),
                  pl.BlockSpec(k1b.shape, lambda i: (0, 0, 0)),
                  pl.BlockSpec(k2b.shape, lambda i: (0, 0, 0))],
        out_specs=[pl.BlockSpec((D, tp), lambda i: (0, i)), tiled, tiled, big, big],
        out_shape=[jax.ShapeDtypeStruct((D, T), BF16),
                   jax.ShapeDtypeStruct((PH, T // TOK_CHUNK, NK, TOK_CHUNK), BF16),
                   jax.ShapeDtypeStruct((PH, T // TOK_CHUNK, NK, TOK_CHUNK), BF16),
                   jax.ShapeDtypeStruct((PH, NK, T), F32),
                   jax.ShapeDtypeStruct((PH, NK, T), F32)],
        compiler_params=_cparams("arbitrary"),
        name="peer_prep",
    )(x1, g_ffn, wqt, k1b, k2b)


def _gelu(x):
    return 0.5 * x * (1.0 + lax.erf(x * (1.0 / math.sqrt(2.0))))


BF16_ROWS = 16
GATE_ROWS = 32


def _row_to_bf16_tile(row, n):
    packed = jnp.broadcast_to(row, (BF16_ROWS, row.shape[1])).astype(BF16)
    return jnp.tile(packed, (n // BF16_ROWS, 1))


def _peer_main_kernel(n_heads, nkeys, rows,
                      x_ref, ht_ref, u_ref, vt_ref, rk_ref, e2_ref, c_ref, n_ref,
                      o_ref, acc_ref, w_ref):
    j = pl.program_id(1)
    tm = ht_ref.shape[1]

    @pl.when(j == 0)
    def _():
        acc_ref[...] = jnp.zeros_like(acc_ref)

    ht = ht_ref[...]
    for r in range(rows):
        act = jnp.dot(u_ref[r * nkeys:(r + 1) * nkeys, :], ht, preferred_element_type=F32)
        for ci in range(tm // TOK_CHUNK):
            tok = slice(ci * TOK_CHUNK, (ci + 1) * TOK_CHUNK)
            g = jnp.zeros((nkeys, TOK_CHUNK), BF16)
            for h in range(n_heads):
                n_b = _row_to_bf16_tile(n_ref[h, r:r + 1, tok], nkeys)
                c_b = _row_to_bf16_tile(c_ref[h, r:r + 1, tok], nkeys)
                g = g + jnp.where(rk_ref[h, ci] < n_b, e2_ref[h, ci] * c_b,
                                  jnp.zeros_like(g))
            w_ref[r * nkeys:(r + 1) * nkeys, tok] = g * _gelu(act[:, tok]).astype(BF16)
    acc_ref[...] += jnp.dot(vt_ref[...], w_ref[...], preferred_element_type=F32)

    @pl.when(j == pl.num_programs(1) - 1)
    def _():
        o_ref[...] = x_ref[...] + acc_ref[...].T


def _peer_main(x1, ht, u, v, rk, e2, c, n, tm):
    T, D = x1.shape
    PH, _, NK, _ = rk.shape
    NE = u.shape[0]
    rows = min(8, NK)
    eb = rows * NK
    ub = u.astype(BF16)
    vt = v.T.astype(BF16)
    tile = pl.BlockSpec((PH, tm // TOK_CHUNK, NK, TOK_CHUNK), lambda i, j: (0, i, 0, 0))
    rowblk = pl.BlockSpec((PH, rows, tm), lambda i, j: (0, j, i))
    return pl.pallas_call(
        functools.partial(_peer_main_kernel, PH, NK, rows),
        grid=(T // tm, NE // eb),
        in_specs=[pl.BlockSpec((tm, D), lambda i, j: (i, 0)),
                  pl.BlockSpec((D, tm), lambda i, j: (0, i)),
                  pl.BlockSpec((eb, D), lambda i, j: (j, 0)),
                  pl.BlockSpec((D, eb), lambda i, j: (0, j)),
                  tile, tile, rowblk, rowblk],
        out_specs=pl.BlockSpec((tm, D), lambda i, j: (i, 0)),
        out_shape=jax.ShapeDtypeStruct((T, D), F32),
        scratch_shapes=[pltpu.VMEM((D, tm), F32), pltpu.VMEM((eb, tm), BF16)],
        compiler_params=_cparams("arbitrary", "arbitrary"),
        name="peer_main",
    )(x1, ht, ub, vt, rk, e2, c, n)


def _ple_out_kernel(final_norm, x_ref, p_ref, gp_ref, wg_ref, wp_ref, gf_ref, o_ref):
    x = x_ref[...]
    h = _rms(x, gp_ref[...]).astype(BF16)
    gate = jax.nn.sigmoid(jnp.dot(h, wg_ref[...], preferred_element_type=F32))
    proj = jnp.dot(p_ref[...].astype(BF16), wp_ref[...], preferred_element_type=F32)
    y = x + gate * proj
    o_ref[...] = _rms(y, gf_ref[...]) if final_norm else y


def _ple_out(x2, p2, g_ple, w_gate, w_proj, g_final, final_norm, tm):
    T, D = x2.shape
    wg = w_gate.astype(BF16)
    wp = w_proj.astype(BF16)

    def full(a):
        return pl.BlockSpec(a.shape, lambda i: (0, 0))

    return pl.pallas_call(
        functools.partial(_ple_out_kernel, final_norm),
        grid=(T // tm,),
        in_specs=[pl.BlockSpec((tm, D), lambda i: (i, 0)),
                  pl.BlockSpec((tm, p2.shape[1]), lambda i: (i, 0)),
                  full(g_ple), full(wg), full(wp), full(g_final)],
        out_specs=pl.BlockSpec((tm, D), lambda i: (i, 0)),
        out_shape=jax.ShapeDtypeStruct((T, D), F32),
        compiler_params=_cparams("arbitrary"),
        name="ple_out",
    )(x2, p2, g_ple, wg, wp, g_final)


def _tile(n, want):
    t = min(n, want)
    assert n % t == 0
    return t


def kernel(x, p, g_mix, w_in, b_f, conv_w, w_branch, w_out, g_ffn, w_peer_q, peer_k1,
           peer_k2, peer_u, peer_v, g_ple, w_ple_gate, w_ple_proj, g_final):
    B, S, D = x.shape
    T = B * S
    depth = w_in.shape[0]
    n_heads = b_f.shape[1]
    aw = w_branch.shape[2]
    cw = conv_w.shape[2]
    tm = _tile(S, 512)
    xt = x.reshape(T, D)
    for i in range(depth):
        q, k, v, gb, z, ga, gv = _in_proj(
            xt, g_mix[i].reshape(1, D), w_in[i], b_f[i], (aw, n_heads, cw), S, tm)
        attn = _fox_attn(q.reshape(B, S, aw), k.reshape(B, S, -1), v.reshape(B, S, aw),
                         n_heads, _tile(S, 512), _tile(S, 1024))
        x1 = _merge(xt, attn.reshape(T, aw), gb, z, conv_w[i], ga, gv, w_branch[i], w_out[i],
                    S, tm)
        ht, rk, e2, cc, nn = _peer_prep(x1, g_ffn[i].reshape(1, D), w_peer_q[i],
                                        peer_k1[i], peer_k2[i], _tile(T, 256))
        x2 = _peer_main(x1, ht, peer_u[i], peer_v[i], rk, e2, cc, nn, _tile(T, 512))
        xt = _ple_out(x2, p[i].reshape(T, p.shape[-1]), g_ple[i].reshape(1, D),
                      w_ple_gate[i], w_ple_proj[i], g_final.reshape(1, D), i == depth - 1, tm)
    return xt.reshape(B, S, D)
```

```python
import functools
import math

import jax
import jax.numpy as jnp
import numpy as np
from jax import lax
from jax.experimental import pallas as pl
from jax.experimental.pallas import tpu as pltpu

EPS = 1e-6
PEER_TOPK = 16
LANES = 128
NEG_BIG = -1e30
REMOVED = 2.0 ** 100
TOK_CHUNK = 256
VMEM_LIMIT = 56 * 1024 * 1024

F32 = jnp.float32
BF16 = jnp.bfloat16


def _cparams(*sem, flags=None):
    return pltpu.CompilerParams(dimension_semantics=sem, vmem_limit_bytes=VMEM_LIMIT,
                                flags=flags)


def _rms(x, g):
    return x * lax.rsqrt(jnp.mean(x * x, axis=-1, keepdims=True) + EPS) * g


def _log_sigmoid(x):
    return jnp.minimum(x, 0.0) - jnp.log1p(jnp.exp(-jnp.abs(x)))


def _split3(x):
    hi = x.astype(BF16)
    r = x - hi.astype(F32)
    mid = r.astype(BF16)
    lo = (r - mid.astype(F32)).astype(BF16)
    return hi, mid, lo


def _in_proj_kernel(tiles_per_seq, n_heads, scale,
                    x_ref, g_ref, wq_ref, wk_ref, wv_ref, wf_ref, bf_ref, wgb_ref, wgc_ref,
                    wxc_ref, wga_ref, wgv_ref, tri_ref, place_ref,
                    q_ref, k_ref, v_ref, gb_ref, z_ref, ga_ref, gv_ref, carry_ref):
    i = pl.program_id(0)
    h = _rms(x_ref[...], g_ref[...]).astype(BF16)

    def mm(w_ref):
        return jnp.dot(h, w_ref[...], preferred_element_type=F32)

    q_ref[...] = (mm(wq_ref) * scale).astype(BF16)
    v_ref[...] = mm(wv_ref).astype(BF16)
    gb_ref[...] = mm(wgb_ref)
    z_ref[...] = mm(wgc_ref) * mm(wxc_ref)
    ga_ref[...] = mm(wga_ref)
    gv_ref[...] = mm(wgv_ref)

    logf = _log_sigmoid(mm(wf_ref) + bf_ref[...])
    lane = lax.broadcasted_iota(jnp.int32, logf.shape, 1)
    logf = jnp.where(lane < n_heads, logf, 0.0)
    tri = tri_ref[...]
    hi, mid, lo = _split3(logf)
    cs = (jnp.dot(tri, hi, preferred_element_type=F32)
          + jnp.dot(tri, mid, preferred_element_type=F32)
          + jnp.dot(tri, lo, preferred_element_type=F32))

    @pl.when(i % tiles_per_seq == 0)
    def _():
        carry_ref[...] = jnp.zeros_like(carry_ref)

    c = cs + carry_ref[...]
    tm = c.shape[0]
    carry_ref[...] = c[tm - 1:tm, :]
    pieces = jnp.concatenate(_split3(-c), axis=1)
    k_ref[...] = (mm(wk_ref) + jnp.dot(pieces, place_ref[...],
                                       preferred_element_type=F32)).astype(BF16)


def _key_layout(n_heads, dh):
    def feat(h, d):
        return (h // 2) * 2 * LANES + (h % 2) * dh + d

    def bias(h, s):
        return (h // 2) * 2 * LANES + LANES + (h % 2) * 3 + s

    return feat, bias


def _in_proj(x2, g_mix, w_in, b_f, sizes, seq, tm):
    T, D = x2.shape
    aw, n_heads, cw = sizes
    offs = [0, aw, 2 * aw, 3 * aw, 3 * aw + n_heads, 3 * aw + n_heads + cw,
            3 * aw + n_heads + 2 * cw, 3 * aw + n_heads + 3 * cw,
            3 * aw + n_heads + 3 * cw + D, 3 * aw + n_heads + 3 * cw + 2 * D]
    seg = [w_in[:, offs[n]:offs[n + 1]] for n in range(9)]
    wq, wk, wv, wf, wgb, wgc, wxc, wga, wgv = seg
    wf = jnp.pad(wf, ((0, 0), (0, LANES - n_heads)))
    bf = jnp.pad(b_f.reshape(1, n_heads), ((0, 0), (0, LANES - n_heads)))
    dh = aw // n_heads
    kw = n_heads * LANES
    wk = jnp.pad(wk.reshape(D, n_heads // 2, 2 * dh),
                 ((0, 0), (0, 0), (0, 2 * LANES - 2 * dh))).reshape(D, kw)
    _, bias_lane = _key_layout(n_heads, dh)
    place = np.zeros((3 * LANES, kw), np.float32)
    for hd in range(n_heads):
        for s in range(3):
            place[s * LANES + hd, bias_lane(hd, s)] = 1.0
    place = jnp.asarray(place, BF16)
    ws = [w.astype(BF16) for w in (wq, wk, wv, wf)] + [bf] + \
         [w.astype(BF16) for w in (wgb, wgc, wxc, wga, wgv)]
    tri = (lax.broadcasted_iota(jnp.int32, (tm, tm), 0)
           >= lax.broadcasted_iota(jnp.int32, (tm, tm), 1)).astype(BF16)
    scale = dh ** -0.5

    def full(a):
        return pl.BlockSpec(a.shape, lambda i: (0, 0))

    def tok(width):
        return pl.BlockSpec((tm, width), lambda i: (i, 0))

    outs = [(aw, BF16), (kw, BF16), (aw, BF16), (cw, F32), (cw, F32), (D, F32), (D, F32)]
    return pl.pallas_call(
        functools.partial(_in_proj_kernel, seq // tm, n_heads, scale),
        grid=(T // tm,),
        in_specs=[tok(D), full(g_mix)] + [full(w) for w in ws] + [full(tri), full(place)],
        out_specs=[tok(w) for w, _ in outs],
        out_shape=[jax.ShapeDtypeStruct((T, w), dt) for w, dt in outs],
        scratch_shapes=[pltpu.VMEM((1, LANES), F32)],
        compiler_params=_cparams("arbitrary"),
        name="in_proj",
    )(x2, g_mix, *ws, tri, place)


def _fox_attn_kernel(tq, tk, dh, q_ref, k_ref, v_ref, o_ref, m_ref, acc_ref):
    qi = pl.program_id(2)
    qp = q_ref[0]
    lane = lax.broadcasted_iota(jnp.int32, qp.shape, 1)
    zero = jnp.zeros_like(qp)
    q_aug = []
    for hh in range(2):
        feat = jnp.where(lane < dh, qp, zero) if hh == 0 else jnp.where(lane >= dh, qp, zero)
        pick = jnp.where(lane < 3 * hh, 0.0, jnp.where(lane < 3 * hh + 3, 1.0, 0.0)).astype(BF16)
        q_aug.append(jnp.concatenate([feat, pick], axis=1))

    m_ref[...] = jnp.full(m_ref.shape, NEG_BIG, F32)
    acc_ref[...] = jnp.zeros(acc_ref.shape, F32)

    def step(k0, width, diag):
        kb = k_ref[0, pl.ds(k0, width), :]
        vb = v_ref[0, pl.ds(k0, width), :]
        vlane = lax.broadcasted_iota(jnp.int32, vb.shape, 1)
        vone = jnp.ones_like(vb)
        v_aug = (jnp.where(vlane < dh, vb, vone), jnp.where(vlane >= dh, vb, vone))
        for hh in range(2):
            s = lax.dot_general(q_aug[hh], kb, (((1,), (1,)), ((), ())),
                                preferred_element_type=F32)
            if diag:
                row = lax.broadcasted_iota(jnp.int32, s.shape, 0)
                col = lax.broadcasted_iota(jnp.int32, s.shape, 1)
                s = jnp.where(col <= row, s, NEG_BIG)
            m_prev = m_ref[hh]
            m_new = jnp.maximum(m_prev, jnp.max(s, axis=1, keepdims=True))
            alpha = jnp.exp(m_prev - m_new)
            p = jnp.exp(s - jnp.tile(m_new, (1, width // LANES)))
            acc_ref[hh] = alpha * acc_ref[hh] + jnp.dot(p.astype(BF16), v_aug[hh],
                                                        preferred_element_type=F32)
            m_ref[hh] = m_new

    def body(j, carry):
        step(pl.multiple_of(j * tk, tk), tk, False)
        return carry

    q0 = qi * tq
    n_wide = q0 // tk
    lax.fori_loop(0, n_wide, body, 0)

    def narrow(j, carry):
        step(pl.multiple_of(n_wide * tk + j * tq, tq), tq, False)
        return carry

    lax.fori_loop(0, (q0 - n_wide * tk) // tq, narrow, 0)
    step(pl.multiple_of(q0, tq), tq, True)

    a0 = acc_ref[0]
    a1 = acc_ref[1]
    out = jnp.where(lane < dh, a0 / pltpu.roll(a0, dh, axis=1), a1 / pltpu.roll(a1, dh, axis=1))
    o_ref[0] = out.astype(o_ref.dtype)


def _fox_attn(q, k, v, n_heads, tq, tk):
    B, S, AW = q.shape
    dh = AW // n_heads
    assert 2 * dh == LANES and n_heads % 2 == 0 and tk % tq == 0 and tk % LANES == 0
    return pl.pallas_call(
        functools.partial(_fox_attn_kernel, tq, tk, dh),
        grid=(B, n_heads // 2, S // tq),
        in_specs=[pl.BlockSpec((1, tq, LANES), lambda b, p, i: (b, i, p)),
                  pl.BlockSpec((1, S, 2 * LANES), lambda b, p, i: (b, 0, p)),
                  pl.BlockSpec((1, S, LANES), lambda b, p, i: (b, 0, p))],
        out_specs=pl.BlockSpec((1, tq, LANES), lambda b, p, i: (b, i, p)),
        out_shape=jax.ShapeDtypeStruct((B, S, AW), BF16),
        scratch_shapes=[pltpu.VMEM((2, tq, LANES), F32), pltpu.VMEM((2, tq, LANES), F32)],
        compiler_params=_cparams("arbitrary", "arbitrary", "arbitrary"),
        name="fox_attn",
    )(q, k, v)


def _merge_kernel(tiles_per_seq, halo,
                  x_ref, a_ref, gb_ref, z_ref, zprev_ref, cw_ref, ga_ref, gv_ref,
                  wua_ref, wuc_ref, wo_ref, o_ref):
    i = pl.program_id(0)
    z = z_ref[...]
    tm = z.shape[0]
    prev = jnp.where(i % tiles_per_seq == 0, 0.0, zprev_ref[...])
    zext = jnp.concatenate([prev, z], axis=0)
    cw = cw_ref[...]
    kk = cw.shape[0]
    conv = z * cw[kk - 1:kk, :]
    for d in range(1, kk):
        conv = conv + pltpu.roll(zext, d, axis=0)[halo:halo + tm] * cw[kk - 1 - d:kk - d, :]
    c = (gb_ref[...] * conv).astype(BF16)
    ya = jnp.dot(a_ref[...], wua_ref[...], preferred_element_type=F32)
    yc = jnp.dot(c, wuc_ref[...], preferred_element_type=F32)
    merged = jax.nn.sigmoid(ga_ref[...]) * ya + jax.nn.sigmoid(gv_ref[...]) * yc
    o_ref[...] = x_ref[...] + jnp.dot(merged.astype(BF16), wo_ref[...],
                                      preferred_element_type=F32)


def _merge(x2, attn, gb, z, conv_w, ga, gv, w_branch, w_out, seq, tm):
    T, D = x2.shape
    aw = attn.shape[1]
    cw = z.shape[1]
    halo = 8
    assert conv_w.shape[0] - 1 <= halo
    wua = w_branch[0].astype(BF16)
    wuc = w_branch[1].astype(BF16)
    wo = w_out.astype(BF16)

    def full(a):
        return pl.BlockSpec(a.shape, lambda i: (0, 0))

    def tok(width):
        return pl.BlockSpec((tm, width), lambda i: (i, 0))

    hb = tm // halo
    return pl.pallas_call(
        functools.partial(_merge_kernel, seq // tm, halo),
        grid=(T // tm,),
        in_specs=[tok(D), tok(aw), tok(cw), tok(cw),
                  pl.BlockSpec((halo, cw), lambda i: (jnp.maximum(i * hb - 1, 0), 0)),
                  full(conv_w), tok(D), tok(D), full(wua), full(wuc), full(wo)],
        out_specs=tok(D),
        out_shape=jax.ShapeDtypeStruct((T, D), F32),
        compiler_params=_cparams("arbitrary"),
        name="merge",
    )(x2, attn, gb, z, z, conv_w, ga, gv, wua, wuc, wo)


def _top_rows(x, k, want_rank=False):
    rows = []
    for r in range(k):
        m = jnp.max(x, axis=0, keepdims=True)
        rows.append(m)
        x = jnp.where(x == m, -REMOVED * (1.0 + r * 2.0 ** -23), x)
    if not want_rank:
        return rows
    rank = jnp.where(x <= -REMOVED, (x * (-1.0 / REMOVED) - 1.0) * 2.0 ** 23, float(k))
    return rows, rank


def _stack_rows(rows, t):
    n = len(rows)
    ridx = lax.broadcasted_iota(jnp.int32, (n, t), 0)
    out = jnp.zeros((n, t), F32)
    for r, row in enumerate(rows):
        out = jnp.where(ridx == r, row, out)
    return out


def _peer_prep_kernel(n_heads, half, topk,
                      x_ref, g_ref, wqt_ref, k1_ref, k2_ref,
                      ht_ref, rk_ref, e2_ref, c_ref, n_ref):
    h2 = _rms(x_ref[...], g_ref[...])
    ht = h2.T.astype(BF16)
    ht_ref[...] = ht
    qt = jnp.dot(wqt_ref[...], ht, preferred_element_type=F32)
    tp = ht.shape[1]
    for h in range(n_heads):
        q1 = qt[h * 2 * half:h * 2 * half + half].astype(BF16)
        q2 = qt[h * 2 * half + half:(h + 1) * 2 * half].astype(BF16)
        s1 = jnp.dot(k1_ref[h], q1, preferred_element_type=F32)
        s2 = jnp.dot(k2_ref[h], q2, preferred_element_type=F32)
        v1, rank1 = _top_rows(s1, topk, True)
        v2, rank2 = _top_rows(s2, topk, True)
        v2a = _stack_rows(v2, tp)
        parts = []
        for a in range(topk):
            nb = topk // (a + 1)
            nb = min(topk, -(-nb // 8) * 8)
            parts.append(v1[a] + v2a[:nb])
        top = _top_rows(jnp.concatenate(parts, axis=0), topk)
        tau = top[topk - 1]
        zsum = jnp.zeros_like(tau)
        for r in range(topk):
            zsum = zsum + jnp.exp(top[r] - top[0])
        n = jnp.zeros_like(s1)
        for a in range(topk):
            n_a = jnp.sum(jnp.where(parts[a] >= tau, 1.0, 0.0), axis=0, keepdims=True)
            n = jnp.where(rank1 == float(a), n_a, n)
        rank2 = rank2.astype(rk_ref.dtype)
        e2 = jnp.exp(s2 - v2[0]).astype(e2_ref.dtype)
        for t in range(tp // TOK_CHUNK):
            rk_ref[h, t] = rank2[:, t * TOK_CHUNK:(t + 1) * TOK_CHUNK]
            e2_ref[h, t] = e2[:, t * TOK_CHUNK:(t + 1) * TOK_CHUNK]
        c_ref[h] = jnp.exp(s1 - v1[0]) / zsum
        n_ref[h] = n


def _peer_prep(x1, g_ffn, w_peer_q, k1, k2, tp):
    T, D = x1.shape
    _, PH, QD = w_peer_q.shape
    NK, HALF = k1.shape[1], k1.shape[2]
    assert NK >= PEER_TOPK and PEER_TOPK % 8 == 0
    wqt = w_peer_q.reshape(D, PH * QD).T.astype(BF16)
    k1b = k1.astype(BF16)
    k2b = k2.astype(BF16)
    big = pl.BlockSpec((PH, NK, tp), lambda i: (0, 0, i))
    tiled = pl.BlockSpec((PH, tp // TOK_CHUNK, NK, TOK_CHUNK), lambda i: (0, i, 0, 0))
    return pl.pallas_call(
        functools.partial(_peer_prep_kernel, PH, HALF, PEER_TOPK),
        grid=(T // tp,),
        in_specs=[pl.BlockSpec((tp, D), lambda i: (i, 0)),
                  pl.BlockSpec(g_ffn.shape, lambda i: (0, 0)),
                  pl.BlockSpec(wqt.shape, lambda i: (0, 0)),
                  pl.BlockSpec(k1b.shape, lambda i: (0, 0, 0)),
                  pl.BlockSpec(k2b.shape, lambda i: (0, 0, 0))],
        out_specs=[pl.BlockSpec((D, tp), lambda i: (0, i)), tiled, tiled, big, big],
        out_shape=[jax.ShapeDtypeStruct((D, T), BF16),
                   jax.ShapeDtypeStruct((PH, T // TOK_CHUNK, NK, TOK_CHUNK), BF16),
                   jax.ShapeDtypeStruct((PH, T // TOK_CHUNK, NK, TOK_CHUNK), BF16),
                   jax.ShapeDtypeStruct((PH, NK, T), F32),
                   jax.ShapeDtypeStruct((PH, NK, T), F32)],
        compiler_params=_cparams("arbitrary"),
        name="peer_prep",
    )(x1, g_ffn, wqt, k1b, k2b)


def _gelu(x):
    return 0.5 * x * (1.0 + lax.erf(x * (1.0 / math.sqrt(2.0))))


BF16_ROWS = 16
GATE_ROWS = 32


def _row_to_bf16_tile(row, n):
    packed = jnp.broadcast_to(row, (BF16_ROWS, row.shape[1])).astype(BF16)
    return jnp.tile(packed, (n // BF16_ROWS, 1))


def _peer_main_kernel(n_heads, nkeys, rows,
                      ht_ref, u_ref, vt_ref, rk_ref, e2_ref, c_ref, n_ref,
                      o_ref, acc_ref, w_ref):
    j = pl.program_id(1)
    tm = ht_ref.shape[1]

    @pl.when(j == 0)
    def _():
        acc_ref[...] = jnp.zeros_like(acc_ref)

    ht = ht_ref[...]
    for r in range(rows):
        act = jnp.dot(u_ref[r * nkeys:(r + 1) * nkeys, :], ht, preferred_element_type=F32)
        for ci in range(tm // TOK_CHUNK):
            tok = slice(ci * TOK_CHUNK, (ci + 1) * TOK_CHUNK)
            g = jnp.zeros((nkeys, TOK_CHUNK), BF16)
            for h in range(n_heads):
                n_b = _row_to_bf16_tile(n_ref[h, r:r + 1, tok], nkeys)
                c_b = _row_to_bf16_tile(c_ref[h, r:r + 1, tok], nkeys)
                g = g + jnp.where(rk_ref[h, ci] < n_b, e2_ref[h, ci] * c_b,
                                  jnp.zeros_like(g))
            w_ref[r * nkeys:(r + 1) * nkeys, tok] = g * _gelu(act[:, tok]).astype(BF16)
    acc_ref[...] += jnp.dot(vt_ref[...], w_ref[...], preferred_element_type=F32)

    @pl.when(j == pl.num_programs(1) - 1)
    def _():
        o_ref[...] = acc_ref[...].T


def _peer_main(ht, u, v, rk, e2, c, n, tm):
    D, T = ht.shape
    PH, _, NK, _ = rk.shape
    NE = u.shape[0]
    rows = min(32, NK)
    eb = rows * NK
    ub = u.astype(BF16)
    vt = v.T.astype(BF16)
    tile = pl.BlockSpec((PH, tm // TOK_CHUNK, NK, TOK_CHUNK), lambda i, j: (0, i, 0, 0))
    rowblk = pl.BlockSpec((PH, rows, tm), lambda i, j: (0, j, i))
    return pl.pallas_call(
        functools.partial(_peer_main_kernel, PH, NK, rows),
        grid=(T // tm, NE // eb),
        in_specs=[pl.BlockSpec((D, tm), lambda i, j: (0, i)),
                  pl.BlockSpec((eb, D), lambda i, j: (j, 0)),
                  pl.BlockSpec((D, eb), lambda i, j: (0, j)),
                  tile, tile, rowblk, rowblk],
        out_specs=pl.BlockSpec((tm, D), lambda i, j: (i, 0)),
        out_shape=jax.ShapeDtypeStruct((T, D), F32),
        scratch_shapes=[pltpu.VMEM((D, tm), F32), pltpu.VMEM((eb, tm), BF16)],
        compiler_params=_cparams("arbitrary", "arbitrary"),
        name="peer_main",
    )(ht, ub, vt, rk, e2, c, n)


def _ple_out_kernel(final_norm, x_ref, f_ref, p_ref, gp_ref, wg_ref, wp_ref, gf_ref, o_ref):
    x = x_ref[...] + f_ref[...]
    h = _rms(x, gp_ref[...]).astype(BF16)
    gate = jax.nn.sigmoid(jnp.dot(h, wg_ref[...], preferred_element_type=F32))
    proj = jnp.dot(p_ref[...].astype(BF16), wp_ref[...], preferred_element_type=F32)
    y = x + gate * proj
    o_ref[...] = _rms(y, gf_ref[...]) if final_norm else y


def _ple_out(x1, ffn, p2, g_ple, w_gate, w_proj, g_final, final_norm, tm):
    T, D = x1.shape
    wg = w_gate.astype(BF16)
    wp = w_proj.astype(BF16)

    def full(a):
        return pl.BlockSpec(a.shape, lambda i: (0, 0))

    return pl.pallas_call(
        functools.partial(_ple_out_kernel, final_norm),
        grid=(T // tm,),
        in_specs=[pl.BlockSpec((tm, D), lambda i: (i, 0)),
                  pl.BlockSpec((tm, D), lambda i: (i, 0)),
                  pl.BlockSpec((tm, p2.shape[1]), lambda i: (i, 0)),
                  full(g_ple), full(wg), full(wp), full(g_final)],
        out_specs=pl.BlockSpec((tm, D), lambda i: (i, 0)),
        out_shape=jax.ShapeDtypeStruct((T, D), F32),
        compiler_params=_cparams("arbitrary"),
        name="ple_out",
    )(x1, ffn, p2, g_ple, wg, wp, g_final)


def _tile(n, want):
    t = min(n, want)
    assert n % t == 0
    return t


def kernel(x, p, g_mix, w_in, b_f, conv_w, w_branch, w_out, g_ffn, w_peer_q, peer_k1,
           peer_k2, peer_u, peer_v, g_ple, w_ple_gate, w_ple_proj, g_final):
    B, S, D = x.shape
    T = B * S
    depth = w_in.shape[0]
    n_heads = b_f.shape[1]
    aw = w_branch.shape[2]
    cw = conv_w.shape[2]
    tm = _tile(S, 512)
    xt = x.reshape(T, D)
    for i in range(depth):
        q, k, v, gb, z, ga, gv = _in_proj(
            xt, g_mix[i].reshape(1, D), w_in[i], b_f[i], (aw, n_heads, cw), S, tm)
        attn = _fox_attn(q.reshape(B, S, aw), k.reshape(B, S, -1), v.reshape(B, S, aw),
                         n_heads, _tile(S, 512), _tile(S, 1024))
        x1 = _merge(xt, attn.reshape(T, aw), gb, z, conv_w[i], ga, gv, w_branch[i], w_out[i],
                    S, tm)
        ht, rk, e2, cc, nn = _peer_prep(x1, g_ffn[i].reshape(1, D), w_peer_q[i],
                                        peer_k1[i], peer_k2[i], _tile(T, 256))
        ffn = _peer_main(ht, peer_u[i], peer_v[i], rk, e2, cc, nn, _tile(T, 512))
        xt = _ple_out(x1, ffn, p[i].reshape(T, p.shape[-1]), g_ple[i].reshape(1, D),
                      w_ple_gate[i], w_ple_proj[i], g_final.reshape(1, D), i == depth - 1, tm)
    return xt.reshape(B, S, D)
```

```python
import functools
import math

import jax
import jax.numpy as jnp
import numpy as np
from jax import lax
from jax.experimental import pallas as pl
from jax.experimental.pallas import tpu as pltpu

EPS = 1e-6
PEER_TOPK = 16
LANES = 128
SUBLANES = 8
NEG_BIG = -1e30
REMOVED = 2.0 ** 100
TOK_CHUNK = 256
VMEM_LIMIT = 56 * 1024 * 1024

F32 = jnp.float32
BF16 = jnp.bfloat16


def _cparams(*sem, flags=None):
    return pltpu.CompilerParams(dimension_semantics=sem, vmem_limit_bytes=VMEM_LIMIT,
                                flags=flags)


def _rms(x, g):
    return x * lax.rsqrt(jnp.mean(x * x, axis=-1, keepdims=True) + EPS) * g


def _log_sigmoid(x):
    return jnp.minimum(x, 0.0) - jnp.log1p(jnp.exp(-jnp.abs(x)))


def _split3(x):
    hi = x.astype(BF16)
    r = x - hi.astype(F32)
    mid = r.astype(BF16)
    lo = (r - mid.astype(F32)).astype(BF16)
    return hi, mid, lo


def _in_proj_kernel(tiles_per_seq, n_heads, scale,
                    x_ref, g_ref, wq_ref, wk_ref, wv_ref, wf_ref, bf_ref, wgb_ref, wgc_ref,
                    wxc_ref, wga_ref, wgv_ref, tri_ref, place_ref,
                    q_ref, k_ref, v_ref, gb_ref, z_ref, ga_ref, gv_ref, carry_ref):
    i = pl.program_id(0)
    h = _rms(x_ref[...], g_ref[...]).astype(BF16)

    def mm(w_ref):
        return jnp.dot(h, w_ref[...], preferred_element_type=F32)

    q_ref[...] = (mm(wq_ref) * scale).astype(BF16)
    v_ref[...] = mm(wv_ref).astype(BF16)
    gb_ref[...] = mm(wgb_ref).astype(BF16)
    z_ref[...] = (mm(wgc_ref) * mm(wxc_ref)).astype(BF16)
    ga_ref[...] = jax.nn.sigmoid(mm(wga_ref)).astype(BF16)
    gv_ref[...] = jax.nn.sigmoid(mm(wgv_ref)).astype(BF16)

    logf = _log_sigmoid(mm(wf_ref) + bf_ref[...])
    lane = lax.broadcasted_iota(jnp.int32, logf.shape, 1)
    logf = jnp.where(lane < n_heads, logf, 0.0)
    tri = tri_ref[...]
    hi, mid, lo = _split3(logf)
    cs = (jnp.dot(tri, hi, preferred_element_type=F32)
          + jnp.dot(tri, mid, preferred_element_type=F32)
          + jnp.dot(tri, lo, preferred_element_type=F32))

    @pl.when(i % tiles_per_seq == 0)
    def _():
        carry_ref[...] = jnp.zeros_like(carry_ref)

    c = cs + carry_ref[...]
    tm = c.shape[0]
    carry_ref[...] = c[tm - 1:tm, :]
    pieces = jnp.concatenate(_split3(-c), axis=1)
    feats = mm(wk_ref).astype(BF16)
    bias = jnp.dot(pieces, place_ref[...], preferred_element_type=F32).astype(BF16)
    for pair in range(n_heads // 2):
        src = slice(pair * LANES, (pair + 1) * LANES)
        k_ref[:, 2 * pair * LANES:(2 * pair + 1) * LANES] = feats[:, src]
        k_ref[:, (2 * pair + 1) * LANES:(2 * pair + 2) * LANES] = bias[:, src]


def _in_proj(x2, g_mix, w_in, b_f, sizes, seq, tm):
    T, D = x2.shape
    aw, n_heads, cw = sizes
    offs = [0, aw, 2 * aw, 3 * aw, 3 * aw + n_heads, 3 * aw + n_heads + cw,
            3 * aw + n_heads + 2 * cw, 3 * aw + n_heads + 3 * cw,
            3 * aw + n_heads + 3 * cw + D, 3 * aw + n_heads + 3 * cw + 2 * D]
    seg = [w_in[:, offs[n]:offs[n + 1]] for n in range(9)]
    wq, wk, wv, wf, wgb, wgc, wxc, wga, wgv = seg
    wf = jnp.pad(wf, ((0, 0), (0, LANES - n_heads)))
    bf = jnp.pad(b_f.reshape(1, n_heads), ((0, 0), (0, LANES - n_heads)))
    dh = aw // n_heads
    assert 2 * dh == LANES
    kw = n_heads * LANES
    place = np.zeros((3 * LANES, kw // 2), np.float32)
    for hd in range(n_heads):
        for s in range(3):
            place[s * LANES + hd, (hd // 2) * LANES + 3 * (hd % 2) + s] = 1.0
    place = jnp.asarray(place, BF16)
    ws = [w.astype(BF16) for w in (wq, wk, wv, wf)] + [bf] + \
         [w.astype(BF16) for w in (wgb, wgc, wxc, wga, wgv)]
    tri = (lax.broadcasted_iota(jnp.int32, (tm, tm), 0)
           >= lax.broadcasted_iota(jnp.int32, (tm, tm), 1)).astype(BF16)
    scale = dh ** -0.5

    def full(a):
        return pl.BlockSpec(a.shape, lambda i: (0, 0))

    def tok(width):
        return pl.BlockSpec((tm, width), lambda i: (i, 0))

    outs = [(aw, BF16), (kw, BF16), (aw, BF16), (cw, BF16), (cw, BF16), (D, BF16), (D, BF16)]
    return pl.pallas_call(
        functools.partial(_in_proj_kernel, seq // tm, n_heads, scale),
        grid=(T // tm,),
        in_specs=[tok(D), full(g_mix)] + [full(w) for w in ws] + [full(tri), full(place)],
        out_specs=[tok(w) for w, _ in outs],
        out_shape=[jax.ShapeDtypeStruct((T, w), dt) for w, dt in outs],
        scratch_shapes=[pltpu.VMEM((1, LANES), F32)],
        compiler_params=_cparams("arbitrary"),
        name="in_proj",
    )(x2, g_mix, *ws, tri, place)


def _fox_attn_kernel(tq, tk, dh, q_ref, k_ref, v_ref, o_ref, m_ref, acc_ref):
    qi = pl.program_id(2)
    qp = q_ref[0]
    lane = lax.broadcasted_iota(jnp.int32, qp.shape, 1)
    zero = jnp.zeros_like(qp)
    q_aug = []
    for hh in range(2):
        feat = jnp.where(lane < dh, qp, zero) if hh == 0 else jnp.where(lane >= dh, qp, zero)
        pick = jnp.where(lane < 3 * hh, 0.0, jnp.where(lane < 3 * hh + 3, 1.0, 0.0)).astype(BF16)
        q_aug.append(jnp.concatenate([feat, pick], axis=1))

    m_ref[...] = jnp.full(m_ref.shape, NEG_BIG, F32)
    acc_ref[...] = jnp.zeros(acc_ref.shape, F32)

    def step(k0, width, diag):
        kb = k_ref[0, pl.ds(k0, width), :]
        vb = v_ref[0, pl.ds(k0, width), :]
        vlane = lax.broadcasted_iota(jnp.int32, vb.shape, 1)
        vone = jnp.ones_like(vb)
        v_aug = (jnp.where(vlane < dh, vb, vone), jnp.where(vlane >= dh, vb, vone))
        for hh in range(2):
            s = lax.dot_general(q_aug[hh], kb, (((1,), (1,)), ((), ())),
                                preferred_element_type=F32)
            if diag:
                row = lax.broadcasted_iota(jnp.int32, s.shape, 0)
                col = lax.broadcasted_iota(jnp.int32, s.shape, 1)
                s = jnp.where(col <= row, s, NEG_BIG)
            m_prev = m_ref[hh]
            m_new = jnp.maximum(m_prev, jnp.max(s, axis=1, keepdims=True))
            alpha = jnp.exp(m_prev - m_new)
            p = jnp.exp(s - jnp.tile(m_new, (1, width // LANES)))
            acc_ref[hh] = alpha * acc_ref[hh] + jnp.dot(p.astype(BF16), v_aug[hh],
                                                        preferred_element_type=F32)
            m_ref[hh] = m_new

    def body(j, carry):
        step(pl.multiple_of(j * tk, tk), tk, False)
        return carry

    q0 = qi * tq
    n_wide = q0 // tk
    lax.fori_loop(0, n_wide, body, 0)

    def narrow(j, carry):
        step(pl.multiple_of(n_wide * tk + j * tq, tq), tq, False)
        return carry

    lax.fori_loop(0, (q0 - n_wide * tk) // tq, narrow, 0)
    step(pl.multiple_of(q0, tq), tq, True)

    a0 = acc_ref[0]
    a1 = acc_ref[1]
    out = jnp.where(lane < dh, a0 / pltpu.roll(a0, dh, axis=1), a1 / pltpu.roll(a1, dh, axis=1))
    o_ref[0] = out.astype(o_ref.dtype)


def _fox_attn(q, k, v, n_heads, tq, tk):
    B, S, AW = q.shape
    dh = AW // n_heads
    assert 2 * dh == LANES and n_heads % 2 == 0 and tk % tq == 0 and tk % LANES == 0
    return pl.pallas_call(
        functools.partial(_fox_attn_kernel, tq, tk, dh),
        grid=(B, n_heads // 2, S // tq),
        in_specs=[pl.BlockSpec((1, tq, LANES), lambda b, p, i: (b, i, p)),
                  pl.BlockSpec((1, S, 2 * LANES), lambda b, p, i: (b, 0, p)),
                  pl.BlockSpec((1, S, LANES), lambda b, p, i: (b, 0, p))],
        out_specs=pl.BlockSpec((1, tq, LANES), lambda b, p, i: (b, i, p)),
        out_shape=jax.ShapeDtypeStruct((B, S, AW), BF16),
        scratch_shapes=[pltpu.VMEM((2, tq, LANES), F32), pltpu.VMEM((2, tq, LANES), F32)],
        compiler_params=_cparams("arbitrary", "arbitrary", "arbitrary"),
        name="fox_attn",
    )(q, k, v)


def _merge_kernel(tiles_per_seq, halo,
                  x_ref, a_ref, gb_ref, z_ref, zprev_ref, cw_ref, ga_ref, gv_ref,
                  wua_ref, wuc_ref, wo_ref, o_ref):
    i = pl.program_id(0)
    z = z_ref[...].astype(F32)
    tm = z.shape[0]
    prev = jnp.where(i % tiles_per_seq == 0, 0.0, zprev_ref[...].astype(F32))
    zext = jnp.concatenate([prev, z], axis=0)
    cw = cw_ref[...]
    kk = cw.shape[0]
    conv = z * cw[kk - 1:kk, :]
    for d in range(1, kk):
        conv = conv + pltpu.roll(zext, d, axis=0)[halo:halo + tm] * cw[kk - 1 - d:kk - d, :]
    c = (gb_ref[...].astype(F32) * conv).astype(BF16)
    ya = jnp.dot(a_ref[...], wua_ref[...], preferred_element_type=F32)
    yc = jnp.dot(c, wuc_ref[...], preferred_element_type=F32)
    merged = ga_ref[...].astype(F32) * ya + gv_ref[...].astype(F32) * yc
    o_ref[...] = x_ref[...] + jnp.dot(merged.astype(BF16), wo_ref[...],
                                      preferred_element_type=F32)


def _merge(x2, attn, gb, z, conv_w, ga, gv, w_branch, w_out, seq, tm):
    T, D = x2.shape
    aw = attn.shape[1]
    cw = z.shape[1]
    halo = 16
    assert conv_w.shape[0] - 1 <= halo
    wua = w_branch[0].astype(BF16)
    wuc = w_branch[1].astype(BF16)
    wo = w_out.astype(BF16)

    def full(a):
        return pl.BlockSpec(a.shape, lambda i: (0, 0))

    def tok(width):
        return pl.BlockSpec((tm, width), lambda i: (i, 0))

    hb = tm // halo
    return pl.pallas_call(
        functools.partial(_merge_kernel, seq // tm, halo),
        grid=(T // tm,),
        in_specs=[tok(D), tok(aw), tok(cw), tok(cw),
                  pl.BlockSpec((halo, cw), lambda i: (jnp.maximum(i * hb - 1, 0), 0)),
                  full(conv_w), tok(D), tok(D), full(wua), full(wuc), full(wo)],
        out_specs=tok(D),
        out_shape=jax.ShapeDtypeStruct((T, D), F32),
        compiler_params=_cparams("arbitrary"),
        name="merge",
    )(x2, attn, gb, z, z, conv_w, ga, gv, wua, wuc, wo)


def _top_rows(x, k, want_rank=False):
    rows = []
    for r in range(k):
        m = jnp.max(x, axis=0, keepdims=True)
        rows.append(m)
        x = jnp.where(x == m, -REMOVED * (1.0 + r * 2.0 ** -23), x)
    if not want_rank:
        return rows
    rank = jnp.where(x <= -REMOVED, (x * (-1.0 / REMOVED) - 1.0) * 2.0 ** 23, float(k))
    return rows, rank


def _ceil_to(n, m):
    return -(-n // m) * m


def _stack_rows(rows, t):
    n = len(rows)
    ridx = lax.broadcasted_iota(jnp.int32, (n, t), 0)
    out = jnp.zeros((n, t), F32)
    for r, row in enumerate(rows):
        out = jnp.where(ridx == r, row, out)
    return out


def _peer_prep_kernel(n_heads, half, topk,
                      x_ref, g_ref, wqt_ref, k1_ref, k2_ref,
                      ht_ref, rk_ref, e2_ref, c_ref, n_ref):
    h2 = _rms(x_ref[...], g_ref[...])
    ht = h2.T.astype(BF16)
    ht_ref[...] = ht
    qt = jnp.dot(wqt_ref[...], ht, preferred_element_type=F32)
    tp = ht.shape[1]
    for h in range(n_heads):
        q1 = qt[h * 2 * half:h * 2 * half + half].astype(BF16)
        q2 = qt[h * 2 * half + half:(h + 1) * 2 * half].astype(BF16)
        s1 = jnp.dot(k1_ref[h], q1, preferred_element_type=F32)
        s2 = jnp.dot(k2_ref[h], q2, preferred_element_type=F32)
        v1, rank1 = _top_rows(s1, topk, True)
        v2, rank2 = _top_rows(s2, topk, True)
        v1a = _stack_rows(v1, tp)
        v2a = _stack_rows(v2, tp)
        few = 4
        cands = [v1[a] + v2a[:_ceil_to(topk // (a + 1), SUBLANES)]
                 for a in range(topk) if topk // (a + 1) >= few]
        cands += [v2[b] + v1a[:_ceil_to(topk // (b + 1), SUBLANES)] for b in range(few - 1)]
        top = _top_rows(jnp.concatenate(cands, axis=0), topk)
        tau = top[topk - 1]
        zsum = jnp.zeros_like(tau)
        for r in range(topk):
            zsum = zsum + jnp.exp(top[r] - top[0])
        n = jnp.zeros_like(s1)
        for a in range(topk):
            nb = _ceil_to(topk // (a + 1), SUBLANES)
            n_a = jnp.sum(jnp.where(v1[a] + v2a[:nb] >= tau, 1.0, 0.0), axis=0, keepdims=True)
            n = jnp.where(rank1 == float(a), n_a, n)
        rank2 = rank2.astype(rk_ref.dtype)
        e2 = jnp.exp(s2 - v2[0]).astype(e2_ref.dtype)
        for t in range(tp // TOK_CHUNK):
            rk_ref[h, t] = rank2[:, t * TOK_CHUNK:(t + 1) * TOK_CHUNK]
            e2_ref[h, t] = e2[:, t * TOK_CHUNK:(t + 1) * TOK_CHUNK]
        c_ref[h] = jnp.exp(s1 - v1[0]) / zsum
        n_ref[h] = n


def _peer_prep(x1, g_ffn, w_peer_q, k1, k2, tp):
    T, D = x1.shape
    _, PH, QD = w_peer_q.shape
    NK, HALF = k1.shape[1], k1.shape[2]
    assert NK >= PEER_TOPK and PEER_TOPK % 8 == 0
    wqt = w_peer_q.reshape(D, PH * QD).T.astype(BF16)
    k1b = k1.astype(BF16)
    k2b = k2.astype(BF16)
    big = pl.BlockSpec((PH, NK, tp), lambda i: (0, 0, i))
    tiled = pl.BlockSpec((PH, tp // TOK_CHUNK, NK, TOK_CHUNK), lambda i: (0, i, 0, 0))
    return pl.pallas_call(
        functools.partial(_peer_prep_kernel, PH, HALF, PEER_TOPK),
        grid=(T // tp,),
        in_specs=[pl.BlockSpec((tp, D), lambda i: (i, 0)),
                  pl.BlockSpec(g_ffn.shape, lambda i: (0, 0)),
                  pl.BlockSpec(wqt.shape, lambda i: (0, 0)),
                  pl.BlockSpec(k1b.shape, lambda i: (0, 0, 0)),
                  pl.BlockSpec(k2b.shape, lambda i: (0, 0, 0))],
        out_specs=[pl.BlockSpec((D, tp), lambda i: (0, i)), tiled, tiled, big, big],
        out_shape=[jax.ShapeDtypeStruct((D, T), BF16),
                   jax.ShapeDtypeStruct((PH, T // TOK_CHUNK, NK, TOK_CHUNK), BF16),
                   jax.ShapeDtypeStruct((PH, T // TOK_CHUNK, NK, TOK_CHUNK), BF16),
                   jax.ShapeDtypeStruct((PH, NK, T), F32),
                   jax.ShapeDtypeStruct((PH, NK, T), F32)],
        compiler_params=_cparams("arbitrary"),
        name="peer_prep",
    )(x1, g_ffn, wqt, k1b, k2b)


def _gelu(x):
    return 0.5 * x * (1.0 + lax.erf(x * (1.0 / math.sqrt(2.0))))


BF16_ROWS = 16


def _row_to_bf16_tile(row, n):
    packed = jnp.broadcast_to(row, (BF16_ROWS, row.shape[1])).astype(BF16)
    return jnp.tile(packed, (n // BF16_ROWS, 1))


def _peer_main_kernel(n_heads, nkeys, rows,
                      ht_ref, u_ref, vt_ref, rk_ref, e2_ref, c_ref, n_ref,
                      o_ref, acc_ref, w_ref):
    j = pl.program_id(1)
    tm = ht_ref.shape[1]

    @pl.when(j == 0)
    def _():
        acc_ref[...] = jnp.zeros_like(acc_ref)

    ht = ht_ref[...]
    for r in range(rows):
        act = jnp.dot(u_ref[r * nkeys:(r + 1) * nkeys, :], ht, preferred_element_type=F32)
        for ci in range(tm // TOK_CHUNK):
            tok = slice(ci * TOK_CHUNK, (ci + 1) * TOK_CHUNK)
            g = jnp.zeros((nkeys, TOK_CHUNK), BF16)
            for h in range(n_heads):
                n_b = _row_to_bf16_tile(n_ref[h, r:r + 1, tok], nkeys)
                c_b = _row_to_bf16_tile(c_ref[h, r:r + 1, tok], nkeys)
                g = g + jnp.where(rk_ref[h, ci] < n_b, e2_ref[h, ci] * c_b,
                                  jnp.zeros_like(g))
            w_ref[r * nkeys:(r + 1) * nkeys, tok] = g * _gelu(act[:, tok]).astype(BF16)
    acc_ref[...] += jnp.dot(vt_ref[...], w_ref[...], preferred_element_type=F32)

    @pl.when(j == pl.num_programs(1) - 1)
    def _():
        o_ref[...] = acc_ref[...].T


def _peer_main(ht, u, v, rk, e2, c, n, tm):
    D, T = ht.shape
    PH, _, NK, _ = rk.shape
    NE = u.shape[0]
    rows = min(32, NK)
    eb = rows * NK
    ub = u.astype(BF16)
    vt = v.T.astype(BF16)
    tile = pl.BlockSpec((PH, tm // TOK_CHUNK, NK, TOK_CHUNK), lambda i, j: (0, i, 0, 0))
    rowblk = pl.BlockSpec((PH, rows, tm), lambda i, j: (0, j, i))
    return pl.pallas_call(
        functools.partial(_peer_main_kernel, PH, NK, rows),
        grid=(T // tm, NE // eb),
        in_specs=[pl.BlockSpec((D, tm), lambda i, j: (0, i)),
                  pl.BlockSpec((eb, D), lambda i, j: (j, 0)),
                  pl.BlockSpec((D, eb), lambda i, j: (0, j)),
                  tile, tile, rowblk, rowblk],
        out_specs=pl.BlockSpec((tm, D), lambda i, j: (i, 0)),
        out_shape=jax.ShapeDtypeStruct((T, D), F32),
        scratch_shapes=[pltpu.VMEM((D, tm), F32), pltpu.VMEM((eb, tm), BF16)],
        compiler_params=_cparams("arbitrary", "arbitrary"),
        name="peer_main",
    )(ht, ub, vt, rk, e2, c, n)


def _ple_out_kernel(final_norm, x_ref, f_ref, p_ref, gp_ref, wg_ref, wp_ref, gf_ref, o_ref):
    x = x_ref[...] + f_ref[...]
    h = _rms(x, gp_ref[...]).astype(BF16)
    gate = jax.nn.sigmoid(jnp.dot(h, wg_ref[...], preferred_element_type=F32))
    proj = jnp.dot(p_ref[...].astype(BF16), wp_ref[...], preferred_element_type=F32)
    y = x + gate * proj
    o_ref[...] = _rms(y, gf_ref[...]) if final_norm else y


def _ple_out(x1, ffn, p2, g_ple, w_gate, w_proj, g_final, final_norm, tm):
    T, D = x1.shape
    wg = w_gate.astype(BF16)
    wp = w_proj.astype(BF16)

    def full(a):
        return pl.BlockSpec(a.shape, lambda i: (0, 0))

    return pl.pallas_call(
        functools.partial(_ple_out_kernel, final_norm),
        grid=(T // tm,),
        in_specs=[pl.BlockSpec((tm, D), lambda i: (i, 0)),
                  pl.BlockSpec((tm, D), lambda i: (i, 0)),
                  pl.BlockSpec((tm, p2.shape[1]), lambda i: (i, 0)),
                  full(g_ple), full(wg), full(wp), full(g_final)],
        out_specs=pl.BlockSpec((tm, D), lambda i: (i, 0)),
        out_shape=jax.ShapeDtypeStruct((T, D), F32),
        compiler_params=_cparams("arbitrary"),
        name="ple_out",
    )(x1, ffn, p2, g_ple, wg, wp, g_final)


def _tile(n, want):
    t = min(n, want)
    assert n % t == 0
    return t


def kernel(x, p, g_mix, w_in, b_f, conv_w, w_branch, w_out, g_ffn, w_peer_q, peer_k1,
           peer_k2, peer_u, peer_v, g_ple, w_ple_gate, w_ple_proj, g_final):
    B, S, D = x.shape
    T = B * S
    depth = w_in.shape[0]
    n_heads = b_f.shape[1]
    aw = w_branch.shape[2]
    cw = conv_w.shape[2]
    tm = _tile(S, 512)
    xt = x.reshape(T, D)
    for i in range(depth):
        q, k, v, gb, z, ga, gv = _in_proj(
            xt, g_mix[i].reshape(1, D), w_in[i], b_f[i], (aw, n_heads, cw), S, tm)
        attn = _fox_attn(q.reshape(B, S, aw), k.reshape(B, S, -1), v.reshape(B, S, aw),
                         n_heads, _tile(S, 512), _tile(S, 1024))
        x1 = _merge(xt, attn.reshape(T, aw), gb, z, conv_w[i], ga, gv, w_branch[i], w_out[i],
                    S, tm)
        ht, rk, e2, cc, nn = _peer_prep(x1, g_ffn[i].reshape(1, D), w_peer_q[i],
                                        peer_k1[i], peer_k2[i], _tile(T, 256))
        ffn = _peer_main(ht, peer_u[i], peer_v[i], rk, e2, cc, nn, _tile(T, 512))
        xt = _ple_out(x1, ffn, p[i].reshape(T, p.shape[-1]), g_ple[i].reshape(1, D),
                      w_ple_gate[i], w_ple_proj[i], g_final.reshape(1, D), i == depth - 1, tm)
    return xt.reshape(B, S, D)
```

```python
import functools
import math

import jax
import jax.numpy as jnp
import numpy as np
from jax import lax
from jax.experimental import pallas as pl
from jax.experimental.pallas import tpu as pltpu

EPS = 1e-6
PEER_TOPK = 16
LANES = 128
SUBLANES = 8
NEG_BIG = -1e30
REMOVED = 2.0 ** 100
TOK_CHUNK = 256
VMEM_LIMIT = 56 * 1024 * 1024

F32 = jnp.float32
BF16 = jnp.bfloat16


def _cparams(*sem, flags=None):
    return pltpu.CompilerParams(dimension_semantics=sem, vmem_limit_bytes=VMEM_LIMIT,
                                flags=flags)


def _rms(x, g):
    return x * lax.rsqrt(jnp.mean(x * x, axis=-1, keepdims=True) + EPS) * g


def _log_sigmoid(x):
    return jnp.minimum(x, 0.0) - jnp.log1p(jnp.exp(-jnp.abs(x)))


def _split3(x):
    hi = x.astype(BF16)
    r = x - hi.astype(F32)
    mid = r.astype(BF16)
    lo = (r - mid.astype(F32)).astype(BF16)
    return hi, mid, lo


def _in_proj_kernel(tiles_per_seq, n_heads, scale,
                    x_ref, g_ref, wq_ref, wk_ref, wv_ref, wf_ref, bf_ref, wgb_ref, wgc_ref,
                    wxc_ref, wga_ref, wgv_ref, tri_ref, place_ref,
                    q_ref, k_ref, v_ref, gb_ref, z_ref, ga_ref, gv_ref, carry_ref):
    i = pl.program_id(0)
    h = _rms(x_ref[...], g_ref[...]).astype(BF16)

    def mm(w_ref):
        return jnp.dot(h, w_ref[...], preferred_element_type=F32)

    q_ref[...] = (mm(wq_ref) * scale).astype(BF16)
    v_ref[...] = mm(wv_ref).astype(BF16)
    gb_ref[...] = mm(wgb_ref).astype(BF16)
    z_ref[...] = (mm(wgc_ref) * mm(wxc_ref)).astype(BF16)
    ga_ref[...] = jax.nn.sigmoid(mm(wga_ref)).astype(BF16)
    gv_ref[...] = jax.nn.sigmoid(mm(wgv_ref)).astype(BF16)

    logf = _log_sigmoid(mm(wf_ref) + bf_ref[...])
    lane = lax.broadcasted_iota(jnp.int32, logf.shape, 1)
    logf = jnp.where(lane < n_heads, logf, 0.0)
    tri = tri_ref[...]
    hi, mid, lo = _split3(logf)
    cs = (jnp.dot(tri, hi, preferred_element_type=F32)
          + jnp.dot(tri, mid, preferred_element_type=F32)
          + jnp.dot(tri, lo, preferred_element_type=F32))

    @pl.when(i % tiles_per_seq == 0)
    def _():
        carry_ref[...] = jnp.zeros_like(carry_ref)

    c = cs + carry_ref[...]
    tm = c.shape[0]
    carry_ref[...] = c[tm - 1:tm, :]
    pieces = jnp.concatenate(_split3(-c), axis=1)
    feats = mm(wk_ref).astype(BF16)
    bias = jnp.dot(pieces, place_ref[...], preferred_element_type=F32).astype(BF16)
    for pair in range(n_heads // 2):
        src = slice(pair * LANES, (pair + 1) * LANES)
        k_ref[:, 2 * pair * LANES:(2 * pair + 1) * LANES] = feats[:, src]
        k_ref[:, (2 * pair + 1) * LANES:(2 * pair + 2) * LANES] = bias[:, src]


def _in_proj(x2, g_mix, w_in, b_f, sizes, seq, tm):
    T, D = x2.shape
    aw, n_heads, cw = sizes
    offs = [0, aw, 2 * aw, 3 * aw, 3 * aw + n_heads, 3 * aw + n_heads + cw,
            3 * aw + n_heads + 2 * cw, 3 * aw + n_heads + 3 * cw,
            3 * aw + n_heads + 3 * cw + D, 3 * aw + n_heads + 3 * cw + 2 * D]
    seg = [w_in[:, offs[n]:offs[n + 1]] for n in range(9)]
    wq, wk, wv, wf, wgb, wgc, wxc, wga, wgv = seg
    wf = jnp.pad(wf, ((0, 0), (0, LANES - n_heads)))
    bf = jnp.pad(b_f.reshape(1, n_heads), ((0, 0), (0, LANES - n_heads)))
    dh = aw // n_heads
    assert 2 * dh == LANES
    kw = n_heads * LANES
    place = np.zeros((3 * LANES, kw // 2), np.float32)
    for hd in range(n_heads):
        for s in range(3):
            place[s * LANES + hd, (hd // 2) * LANES + 3 * (hd % 2) + s] = 1.0
    place = jnp.asarray(place, BF16)
    ws = [w.astype(BF16) for w in (wq, wk, wv, wf)] + [bf] + \
         [w.astype(BF16) for w in (wgb, wgc, wxc, wga, wgv)]
    tri = (lax.broadcasted_iota(jnp.int32, (tm, tm), 0)
           >= lax.broadcasted_iota(jnp.int32, (tm, tm), 1)).astype(BF16)
    scale = dh ** -0.5

    def full(a):
        return pl.BlockSpec(a.shape, lambda i: (0, 0))

    def tok(width):
        return pl.BlockSpec((tm, width), lambda i: (i, 0))

    outs = [(aw, BF16), (kw, BF16), (aw, BF16), (cw, BF16), (cw, BF16), (D, BF16), (D, BF16)]
    return pl.pallas_call(
        functools.partial(_in_proj_kernel, seq // tm, n_heads, scale),
        grid=(T // tm,),
        in_specs=[tok(D), full(g_mix)] + [full(w) for w in ws] + [full(tri), full(place)],
        out_specs=[tok(w) for w, _ in outs],
        out_shape=[jax.ShapeDtypeStruct((T, w), dt) for w, dt in outs],
        scratch_shapes=[pltpu.VMEM((1, LANES), F32)],
        compiler_params=_cparams("arbitrary"),
        name="in_proj",
    )(x2, g_mix, *ws, tri, place)


def _fox_attn_kernel(tq, tk, dh, q_ref, k_ref, v_ref, o_ref, m_ref, acc_ref):
    qi = pl.program_id(2)
    qp = q_ref[0]
    lane = lax.broadcasted_iota(jnp.int32, qp.shape, 1)
    zero = jnp.zeros_like(qp)
    q_aug = []
    for hh in range(2):
        feat = jnp.where(lane < dh, qp, zero) if hh == 0 else jnp.where(lane >= dh, qp, zero)
        pick = jnp.where(lane < 3 * hh, 0.0, jnp.where(lane < 3 * hh + 3, 1.0, 0.0)).astype(BF16)
        q_aug.append(jnp.concatenate([feat, pick], axis=1))

    m_ref[...] = jnp.full(m_ref.shape, NEG_BIG, F32)
    acc_ref[...] = jnp.zeros(acc_ref.shape, F32)

    def step(k0, width, diag):
        kb = k_ref[0, pl.ds(k0, width), :]
        vb = v_ref[0, pl.ds(k0, width), :]
        vlane = lax.broadcasted_iota(jnp.int32, vb.shape, 1)
        vone = jnp.ones_like(vb)
        v_aug = (jnp.where(vlane < dh, vb, vone), jnp.where(vlane >= dh, vb, vone))
        for hh in range(2):
            s = lax.dot_general(q_aug[hh], kb, (((1,), (1,)), ((), ())),
                                preferred_element_type=F32)
            if diag:
                row = lax.broadcasted_iota(jnp.int32, s.shape, 0)
                col = lax.broadcasted_iota(jnp.int32, s.shape, 1)
                s = jnp.where(col <= row, s, NEG_BIG)
            m_prev = m_ref[hh]
            m_new = jnp.maximum(m_prev, jnp.max(s, axis=1, keepdims=True))
            alpha = jnp.exp(m_prev - m_new)
            p = jnp.exp(s - jnp.tile(m_new, (1, width // LANES)))
            acc_ref[hh] = alpha * acc_ref[hh] + jnp.dot(p.astype(BF16), v_aug[hh],
                                                        preferred_element_type=F32)
            m_ref[hh] = m_new

    def body(j, carry):
        step(pl.multiple_of(j * tk, tk), tk, False)
        return carry

    q0 = qi * tq
    n_wide = q0 // tk
    lax.fori_loop(0, n_wide, body, 0)

    def narrow(j, carry):
        step(pl.multiple_of(n_wide * tk + j * tq, tq), tq, False)
        return carry

    lax.fori_loop(0, (q0 - n_wide * tk) // tq, narrow, 0)
    step(pl.multiple_of(q0, tq), tq, True)

    a0 = acc_ref[0]
    a1 = acc_ref[1]
    out = jnp.where(lane < dh, a0 / pltpu.roll(a0, dh, axis=1), a1 / pltpu.roll(a1, dh, axis=1))
    o_ref[0] = out.astype(o_ref.dtype)


def _fox_attn(q, k, v, n_heads, tq, tk):
    B, S, AW = q.shape
    dh = AW // n_heads
    assert 2 * dh == LANES and n_heads % 2 == 0 and tk % tq == 0 and tk % LANES == 0
    return pl.pallas_call(
        functools.partial(_fox_attn_kernel, tq, tk, dh),
        grid=(B, n_heads // 2, S // tq),
        in_specs=[pl.BlockSpec((1, tq, LANES), lambda b, p, i: (b, i, p)),
                  pl.BlockSpec((1, S, 2 * LANES), lambda b, p, i: (b, 0, p)),
                  pl.BlockSpec((1, S, LANES), lambda b, p, i: (b, 0, p))],
        out_specs=pl.BlockSpec((1, tq, LANES), lambda b, p, i: (b, i, p)),
        out_shape=jax.ShapeDtypeStruct((B, S, AW), BF16),
        scratch_shapes=[pltpu.VMEM((2, tq, LANES), F32), pltpu.VMEM((2, tq, LANES), F32)],
        compiler_params=_cparams("arbitrary", "arbitrary", "arbitrary"),
        name="fox_attn",
    )(q, k, v)


def _merge_kernel(tiles_per_seq, halo,
                  x_ref, a_ref, gb_ref, z_ref, zprev_ref, cw_ref, ga_ref, gv_ref,
                  wua_ref, wuc_ref, wo_ref, o_ref):
    i = pl.program_id(0)
    z = z_ref[...].astype(F32)
    tm = z.shape[0]
    prev = jnp.where(i % tiles_per_seq == 0, 0.0, zprev_ref[...].astype(F32))
    zext = jnp.concatenate([prev, z], axis=0)
    cw = cw_ref[...]
    kk = cw.shape[0]
    conv = z * cw[kk - 1:kk, :]
    for d in range(1, kk):
        conv = conv + pltpu.roll(zext, d, axis=0)[halo:halo + tm] * cw[kk - 1 - d:kk - d, :]
    c = (gb_ref[...].astype(F32) * conv).astype(BF16)
    ya = jnp.dot(a_ref[...], wua_ref[...], preferred_element_type=F32)
    yc = jnp.dot(c, wuc_ref[...], preferred_element_type=F32)
    merged = ga_ref[...].astype(F32) * ya + gv_ref[...].astype(F32) * yc
    o_ref[...] = x_ref[...] + jnp.dot(merged.astype(BF16), wo_ref[...],
                                      preferred_element_type=F32)


def _merge(x2, attn, gb, z, conv_w, ga, gv, w_branch, w_out, seq, tm):
    T, D = x2.shape
    aw = attn.shape[1]
    cw = z.shape[1]
    halo = 16
    assert conv_w.shape[0] - 1 <= halo
    wua = w_branch[0].astype(BF16)
    wuc = w_branch[1].astype(BF16)
    wo = w_out.astype(BF16)

    def full(a):
        return pl.BlockSpec(a.shape, lambda i: (0, 0))

    def tok(width):
        return pl.BlockSpec((tm, width), lambda i: (i, 0))

    hb = tm // halo
    return pl.pallas_call(
        functools.partial(_merge_kernel, seq // tm, halo),
        grid=(T // tm,),
        in_specs=[tok(D), tok(aw), tok(cw), tok(cw),
                  pl.BlockSpec((halo, cw), lambda i: (jnp.maximum(i * hb - 1, 0), 0)),
                  full(conv_w), tok(D), tok(D), full(wua), full(wuc), full(wo)],
        out_specs=tok(D),
        out_shape=jax.ShapeDtypeStruct((T, D), F32),
        compiler_params=_cparams("arbitrary"),
        name="merge",
    )(x2, attn, gb, z, z, conv_w, ga, gv, wua, wuc, wo)


def _top_rows(x, k, want_rank=False):
    rows = []
    for r in range(k):
        m = jnp.max(x, axis=0, keepdims=True)
        rows.append(m)
        x = jnp.where(x == m, -REMOVED * (1.0 + r * 2.0 ** -23), x)
    if not want_rank:
        return rows
    rank = jnp.where(x <= -REMOVED, (x * (-1.0 / REMOVED) - 1.0) * 2.0 ** 23, float(k))
    return rows, rank


def _ceil_to(n, m):
    return -(-n // m) * m


def _stack_rows(rows, t):
    n = len(rows)
    ridx = lax.broadcasted_iota(jnp.int32, (n, t), 0)
    out = jnp.zeros((n, t), F32)
    for r, row in enumerate(rows):
        out = jnp.where(ridx == r, row, out)
    return out


def _peer_prep_kernel(n_heads, half, topk,
                      x_ref, g_ref, wqt_ref, k1_ref, k2_ref, u_ref, v_ref,
                      ht_ref, rk_ref, e2_ref, c_ref, n_ref, ub_ref, vt_ref):
    ub_ref[...] = u_ref[...].astype(BF16)
    vt_ref[...] = v_ref[...].T.astype(BF16)
    h2 = _rms(x_ref[...], g_ref[...])
    ht = h2.T.astype(BF16)
    ht_ref[...] = ht
    qt = jnp.dot(wqt_ref[...], ht, preferred_element_type=F32)
    tp = ht.shape[1]
    for h in range(n_heads):
        q1 = qt[h * 2 * half:h * 2 * half + half].astype(BF16)
        q2 = qt[h * 2 * half + half:(h + 1) * 2 * half].astype(BF16)
        s1 = jnp.dot(k1_ref[h], q1, preferred_element_type=F32)
        s2 = jnp.dot(k2_ref[h], q2, preferred_element_type=F32)
        v1, rank1 = _top_rows(s1, topk, True)
        v2, rank2 = _top_rows(s2, topk, True)
        v1a = _stack_rows(v1, tp)
        v2a = _stack_rows(v2, tp)
        few = 4
        cands = [v1[a] + v2a[:_ceil_to(topk // (a + 1), SUBLANES)]
                 for a in range(topk) if topk // (a + 1) >= few]
        cands += [v2[b] + v1a[:_ceil_to(topk // (b + 1), SUBLANES)] for b in range(few - 1)]
        top = _top_rows(jnp.concatenate(cands, axis=0), topk)
        tau = top[topk - 1]
        zsum = jnp.zeros_like(tau)
        for r in range(topk):
            zsum = zsum + jnp.exp(top[r] - top[0])
        rank1 = rank1.astype(BF16)
        n = jnp.zeros(s1.shape, BF16)
        for a in range(topk):
            nb = _ceil_to(topk // (a + 1), SUBLANES)
            n_a = jnp.sum(jnp.where(v1[a] + v2a[:nb] >= tau, 1.0, 0.0), axis=0, keepdims=True)
            n = jnp.where(rank1 == float(a), _row_to_bf16_tile(n_a, s1.shape[0]), n)
        rank2 = rank2.astype(rk_ref.dtype)
        e2 = jnp.exp(s2 - v2[0]).astype(e2_ref.dtype)
        for t in range(tp // TOK_CHUNK):
            rk_ref[h, t] = rank2[:, t * TOK_CHUNK:(t + 1) * TOK_CHUNK]
            e2_ref[h, t] = e2[:, t * TOK_CHUNK:(t + 1) * TOK_CHUNK]
        c_ref[h] = jnp.exp(s1 - v1[0]) / zsum
        n_ref[h] = n.astype(F32)


def _peer_prep(x1, g_ffn, w_peer_q, k1, k2, u, v, tp):
    T, D = x1.shape
    _, PH, QD = w_peer_q.shape
    NK, HALF = k1.shape[1], k1.shape[2]
    NE = u.shape[0]
    assert NK >= PEER_TOPK and PEER_TOPK % 8 == 0
    steps = T // tp
    es = NE // steps
    assert NE % steps == 0 and es % LANES == 0
    wqt = w_peer_q.reshape(D, PH * QD).T.astype(BF16)
    k1b = k1.astype(BF16)
    k2b = k2.astype(BF16)
    big = pl.BlockSpec((PH, NK, tp), lambda i: (0, 0, i))
    tiled = pl.BlockSpec((PH, tp // TOK_CHUNK, NK, TOK_CHUNK), lambda i: (0, i, 0, 0))
    rows = pl.BlockSpec((es, D), lambda i: (i, 0))
    return pl.pallas_call(
        functools.partial(_peer_prep_kernel, PH, HALF, PEER_TOPK),
        grid=(steps,),
        in_specs=[pl.BlockSpec((tp, D), lambda i: (i, 0)),
                  pl.BlockSpec(g_ffn.shape, lambda i: (0, 0)),
                  pl.BlockSpec(wqt.shape, lambda i: (0, 0)),
                  pl.BlockSpec(k1b.shape, lambda i: (0, 0, 0)),
                  pl.BlockSpec(k2b.shape, lambda i: (0, 0, 0)), rows, rows],
        out_specs=[pl.BlockSpec((D, tp), lambda i: (0, i)), tiled, tiled, big, big, rows,
                   pl.BlockSpec((D, es), lambda i: (0, i))],
        out_shape=[jax.ShapeDtypeStruct((D, T), BF16),
                   jax.ShapeDtypeStruct((PH, T // TOK_CHUNK, NK, TOK_CHUNK), BF16),
                   jax.ShapeDtypeStruct((PH, T // TOK_CHUNK, NK, TOK_CHUNK), BF16),
                   jax.ShapeDtypeStruct((PH, NK, T), F32),
                   jax.ShapeDtypeStruct((PH, NK, T), F32),
                   jax.ShapeDtypeStruct((NE, D), BF16),
                   jax.ShapeDtypeStruct((D, NE), BF16)],
        compiler_params=_cparams("arbitrary"),
        name="peer_prep",
    )(x1, g_ffn, wqt, k1b, k2b, u, v)


def _gelu(x):
    return 0.5 * x * (1.0 + lax.erf(x * (1.0 / math.sqrt(2.0))))


BF16_ROWS = 16


def _row_to_bf16_tile(row, n):
    packed = jnp.broadcast_to(row, (BF16_ROWS, row.shape[1])).astype(BF16)
    return jnp.tile(packed, (n // BF16_ROWS, 1))


def _peer_main_kernel(n_heads, nkeys, rows,
                      ht_ref, u_ref, vt_ref, rk_ref, e2_ref, c_ref, n_ref,
                      o_ref, acc_ref, w_ref):
    j = pl.program_id(1)
    tm = ht_ref.shape[1]

    @pl.when(j == 0)
    def _():
        acc_ref[...] = jnp.zeros_like(acc_ref)

    ht = ht_ref[...]
    for r in range(rows):
        act = jnp.dot(u_ref[r * nkeys:(r + 1) * nkeys, :], ht, preferred_element_type=F32)
        for ci in range(tm // TOK_CHUNK):
            tok = slice(ci * TOK_CHUNK, (ci + 1) * TOK_CHUNK)
            g = jnp.zeros((nkeys, TOK_CHUNK), BF16)
            for h in range(n_heads):
                n_b = _row_to_bf16_tile(n_ref[h, r:r + 1, tok], nkeys)
                c_b = _row_to_bf16_tile(c_ref[h, r:r + 1, tok], nkeys)
                g = g + jnp.where(rk_ref[h, ci] < n_b, e2_ref[h, ci] * c_b,
                                  jnp.zeros_like(g))
            w_ref[r * nkeys:(r + 1) * nkeys, tok] = g * _gelu(act[:, tok]).astype(BF16)
    acc_ref[...] += jnp.dot(vt_ref[...], w_ref[...], preferred_element_type=F32)

    @pl.when(j == pl.num_programs(1) - 1)
    def _():
        o_ref[...] = acc_ref[...].T


def _peer_main(ht, ub, vt, rk, e2, c, n, tm):
    D, T = ht.shape
    PH, _, NK, _ = rk.shape
    NE = ub.shape[0]
    rows = min(32, NK)
    eb = rows * NK
    tile = pl.BlockSpec((PH, tm // TOK_CHUNK, NK, TOK_CHUNK), lambda i, j: (0, i, 0, 0))
    rowblk = pl.BlockSpec((PH, rows, tm), lambda i, j: (0, j, i))
    return pl.pallas_call(
        functools.partial(_peer_main_kernel, PH, NK, rows),
        grid=(T // tm, NE // eb),
        in_specs=[pl.BlockSpec((D, tm), lambda i, j: (0, i)),
                  pl.BlockSpec((eb, D), lambda i, j: (j, 0)),
                  pl.BlockSpec((D, eb), lambda i, j: (0, j)),
                  tile, tile, rowblk, rowblk],
        out_specs=pl.BlockSpec((tm, D), lambda i, j: (i, 0)),
        out_shape=jax.ShapeDtypeStruct((T, D), F32),
        scratch_shapes=[pltpu.VMEM((D, tm), F32), pltpu.VMEM((eb, tm), BF16)],
        compiler_params=_cparams("arbitrary", "arbitrary"),
        name="peer_main",
    )(ht, ub, vt, rk, e2, c, n)


def _ple_out_kernel(final_norm, x_ref, f_ref, p_ref, gp_ref, wg_ref, wp_ref, gf_ref, o_ref):
    x = x_ref[...] + f_ref[...]
    h = _rms(x, gp_ref[...]).astype(BF16)
    gate = jax.nn.sigmoid(jnp.dot(h, wg_ref[...], preferred_element_type=F32))
    proj = jnp.dot(p_ref[...].astype(BF16), wp_ref[...], preferred_element_type=F32)
    y = x + gate * proj
    o_ref[...] = _rms(y, gf_ref[...]) if final_norm else y


def _ple_out(x1, ffn, p2, g_ple, w_gate, w_proj, g_final, final_norm, tm):
    T, D = x1.shape
    wg = w_gate.astype(BF16)
    wp = w_proj.astype(BF16)

    def full(a):
        return pl.BlockSpec(a.shape, lambda i: (0, 0))

    return pl.pallas_call(
        functools.partial(_ple_out_kernel, final_norm),
        grid=(T // tm,),
        in_specs=[pl.BlockSpec((tm, D), lambda i: (i, 0)),
                  pl.BlockSpec((tm, D), lambda i: (i, 0)),
                  pl.BlockSpec((tm, p2.shape[1]), lambda i: (i, 0)),
                  full(g_ple), full(wg), full(wp), full(g_final)],
        out_specs=pl.BlockSpec((tm, D), lambda i: (i, 0)),
        out_shape=jax.ShapeDtypeStruct((T, D), F32),
        compiler_params=_cparams("arbitrary"),
        name="ple_out",
    )(x1, ffn, p2, g_ple, wg, wp, g_final)


def _tile(n, want):
    t = min(n, want)
    assert n % t == 0
    return t


def kernel(x, p, g_mix, w_in, b_f, conv_w, w_branch, w_out, g_ffn, w_peer_q, peer_k1,
           peer_k2, peer_u, peer_v, g_ple, w_ple_gate, w_ple_proj, g_final):
    B, S, D = x.shape
    T = B * S
    depth = w_in.shape[0]
    n_heads = b_f.shape[1]
    aw = w_branch.shape[2]
    cw = conv_w.shape[2]
    tm = _tile(S, 512)
    xt = x.reshape(T, D)
    for i in range(depth):
        q, k, v, gb, z, ga, gv = _in_proj(
            xt, g_mix[i].reshape(1, D), w_in[i], b_f[i], (aw, n_heads, cw), S, tm)
        attn = _fox_attn(q.reshape(B, S, aw), k.reshape(B, S, -1), v.reshape(B, S, aw),
                         n_heads, _tile(S, 512), _tile(S, 1024))
        x1 = _merge(xt, attn.reshape(T, aw), gb, z, conv_w[i], ga, gv, w_branch[i], w_out[i],
                    S, tm)
        ht, rk, e2, cc, nn, ub, vt = _peer_prep(x1, g_ffn[i].reshape(1, D), w_peer_q[i],
                                                peer_k1[i], peer_k2[i], peer_u[i], peer_v[i],
                                                _tile(T, 256))
        ffn = _peer_main(ht, ub, vt, rk, e2, cc, nn, _tile(T, 512))
        xt = _ple_out(x1, ffn, p[i].reshape(T, p.shape[-1]), g_ple[i].reshape(1, D),
                      w_ple_gate[i], w_ple_proj[i], g_final.reshape(1, D), i == depth - 1, tm)
    return xt.reshape(B, S, D)
```

```python
import functools
import math

import jax
import jax.numpy as jnp
import numpy as np
from jax import lax
from jax.experimental import pallas as pl
from jax.experimental.pallas import tpu as pltpu

EPS = 1e-6
PEER_TOPK = 16
LANES = 128
SUBLANES = 8
NEG_BIG = -1e30
REMOVED = 2.0 ** 100
TOK_CHUNK = 256
VMEM_LIMIT = 56 * 1024 * 1024

F32 = jnp.float32
BF16 = jnp.bfloat16


def _cparams(*sem, flags=None):
    return pltpu.CompilerParams(dimension_semantics=sem, vmem_limit_bytes=VMEM_LIMIT,
                                flags=flags)


def _rms(x, g):
    return x * lax.rsqrt(jnp.mean(x * x, axis=-1, keepdims=True) + EPS) * g


def _log_sigmoid(x):
    return jnp.minimum(x, 0.0) - jnp.log1p(jnp.exp(-jnp.abs(x)))


def _split3(x):
    hi = x.astype(BF16)
    r = x - hi.astype(F32)
    mid = r.astype(BF16)
    lo = (r - mid.astype(F32)).astype(BF16)
    return hi, mid, lo


def _in_proj_kernel(tiles_per_seq, n_heads, scale,
                    x_ref, g_ref, wq_ref, wk_ref, wv_ref, wf_ref, bf_ref, wgb_ref, wgc_ref,
                    wxc_ref, wga_ref, wgv_ref, tri_ref, place_ref,
                    q_ref, k_ref, v_ref, gb_ref, z_ref, ga_ref, gv_ref, carry_ref):
    i = pl.program_id(0)
    h = _rms(x_ref[...], g_ref[...]).astype(BF16)

    def mm(w_ref):
        return jnp.dot(h, w_ref[...], preferred_element_type=F32)

    q_ref[...] = (mm(wq_ref) * scale).astype(BF16)
    v_ref[...] = mm(wv_ref).astype(BF16)
    gb_ref[...] = mm(wgb_ref).astype(BF16)
    z_ref[...] = (mm(wgc_ref) * mm(wxc_ref)).astype(BF16)
    ga_ref[...] = jax.nn.sigmoid(mm(wga_ref)).astype(BF16)
    gv_ref[...] = jax.nn.sigmoid(mm(wgv_ref)).astype(BF16)

    logf = _log_sigmoid(mm(wf_ref) + bf_ref[...])
    lane = lax.broadcasted_iota(jnp.int32, logf.shape, 1)
    logf = jnp.where(lane < n_heads, logf, 0.0)
    tri = tri_ref[...]
    hi, mid, lo = _split3(logf)
    cs = (jnp.dot(tri, hi, preferred_element_type=F32)
          + jnp.dot(tri, mid, preferred_element_type=F32)
          + jnp.dot(tri, lo, preferred_element_type=F32))

    @pl.when(i % tiles_per_seq == 0)
    def _():
        carry_ref[...] = jnp.zeros_like(carry_ref)

    c = cs + carry_ref[...]
    tm = c.shape[0]
    carry_ref[...] = c[tm - 1:tm, :]
    pieces = jnp.concatenate(_split3(-c), axis=1)
    feats = mm(wk_ref).astype(BF16)
    bias = jnp.dot(pieces, place_ref[...], preferred_element_type=F32).astype(BF16)
    for pair in range(n_heads // 2):
        src = slice(pair * LANES, (pair + 1) * LANES)
        k_ref[:, 2 * pair * LANES:(2 * pair + 1) * LANES] = feats[:, src]
        k_ref[:, (2 * pair + 1) * LANES:(2 * pair + 2) * LANES] = bias[:, src]


def _in_proj(x2, g_mix, w_in, b_f, sizes, seq, tm):
    T, D = x2.shape
    aw, n_heads, cw = sizes
    offs = [0, aw, 2 * aw, 3 * aw, 3 * aw + n_heads, 3 * aw + n_heads + cw,
            3 * aw + n_heads + 2 * cw, 3 * aw + n_heads + 3 * cw,
            3 * aw + n_heads + 3 * cw + D, 3 * aw + n_heads + 3 * cw + 2 * D]
    seg = [w_in[:, offs[n]:offs[n + 1]] for n in range(9)]
    wq, wk, wv, wf, wgb, wgc, wxc, wga, wgv = seg
    wf = jnp.pad(wf, ((0, 0), (0, LANES - n_heads)))
    bf = jnp.pad(b_f.reshape(1, n_heads), ((0, 0), (0, LANES - n_heads)))
    dh = aw // n_heads
    assert 2 * dh == LANES
    kw = n_heads * LANES
    place = np.zeros((3 * LANES, kw // 2), np.float32)
    for hd in range(n_heads):
        for s in range(3):
            place[s * LANES + hd, (hd // 2) * LANES + 3 * (hd % 2) + s] = 1.0
    place = jnp.asarray(place, BF16)
    ws = [w.astype(BF16) for w in (wq, wk, wv, wf)] + [bf] + \
         [w.astype(BF16) for w in (wgb, wgc, wxc, wga, wgv)]
    tri = (lax.broadcasted_iota(jnp.int32, (tm, tm), 0)
           >= lax.broadcasted_iota(jnp.int32, (tm, tm), 1)).astype(BF16)
    scale = dh ** -0.5

    def full(a):
        return pl.BlockSpec(a.shape, lambda i: (0, 0))

    def tok(width):
        return pl.BlockSpec((tm, width), lambda i: (i, 0))

    outs = [(aw, BF16), (kw, BF16), (aw, BF16), (cw, BF16), (cw, BF16), (D, BF16), (D, BF16)]
    return pl.pallas_call(
        functools.partial(_in_proj_kernel, seq // tm, n_heads, scale),
        grid=(T // tm,),
        in_specs=[tok(D), full(g_mix)] + [full(w) for w in ws] + [full(tri), full(place)],
        out_specs=[tok(w) for w, _ in outs],
        out_shape=[jax.ShapeDtypeStruct((T, w), dt) for w, dt in outs],
        scratch_shapes=[pltpu.VMEM((1, LANES), F32)],
        compiler_params=_cparams("arbitrary"),
        name="in_proj",
    )(x2, g_mix, *ws, tri, place)


def _fox_attn_kernel(tq, tk, dh, q_ref, k_ref, v_ref, o_ref, m_ref, acc_ref):
    qi = pl.program_id(2)
    qp = q_ref[0]
    lane = lax.broadcasted_iota(jnp.int32, qp.shape, 1)
    zero = jnp.zeros_like(qp)
    q_aug = []
    for hh in range(2):
        feat = jnp.where(lane < dh, qp, zero) if hh == 0 else jnp.where(lane >= dh, qp, zero)
        pick = jnp.where(lane < 3 * hh, 0.0, jnp.where(lane < 3 * hh + 3, 1.0, 0.0)).astype(BF16)
        q_aug.append(jnp.concatenate([feat, pick], axis=1))

    m_ref[...] = jnp.full(m_ref.shape, NEG_BIG, F32)
    acc_ref[...] = jnp.zeros(acc_ref.shape, F32)

    def step(k0, width, lead):
        kb = k_ref[0, pl.ds(k0, width), :]
        vb = v_ref[0, pl.ds(k0, width), :]
        vlane = lax.broadcasted_iota(jnp.int32, vb.shape, 1)
        vone = jnp.ones_like(vb)
        v_aug = (jnp.where(vlane < dh, vb, vone), jnp.where(vlane >= dh, vb, vone))
        for hh in range(2):
            s = lax.dot_general(q_aug[hh], kb, (((1,), (1,)), ((), ())),
                                preferred_element_type=F32)
            if lead is not None:
                row = lax.broadcasted_iota(jnp.int32, (tq, width - lead), 0)
                col = lax.broadcasted_iota(jnp.int32, (tq, width - lead), 1)
                tail = jnp.where(col <= row, s[:, lead:], NEG_BIG)
                s = tail if lead == 0 else jnp.concatenate([s[:, :lead], tail], axis=1)
            m_prev = m_ref[hh]
            m_new = jnp.maximum(m_prev, jnp.max(s, axis=1, keepdims=True))
            alpha = jnp.exp(m_prev - m_new)
            p = jnp.exp(s - jnp.tile(m_new, (1, width // LANES)))
            acc_ref[hh] = alpha * acc_ref[hh] + jnp.dot(p.astype(BF16), v_aug[hh],
                                                        preferred_element_type=F32)
            m_ref[hh] = m_new

    def body(j, carry):
        step(pl.multiple_of(j * tk, tk), tk, None)
        return carry

    q0 = qi * tq
    n_wide = q0 // tk
    lax.fori_loop(0, n_wide, body, 0)
    k_last = pl.multiple_of(n_wide * tk, tk)
    for m in range(tk // tq):
        @pl.when(q0 - n_wide * tk == m * tq)
        def _():
            step(k_last, (m + 1) * tq, m * tq)

    a0 = acc_ref[0]
    a1 = acc_ref[1]
    out = jnp.where(lane < dh, a0 / pltpu.roll(a0, dh, axis=1), a1 / pltpu.roll(a1, dh, axis=1))
    o_ref[0] = out.astype(o_ref.dtype)


def _fox_attn(q, k, v, n_heads, tq, tk):
    B, S, AW = q.shape
    dh = AW // n_heads
    assert 2 * dh == LANES and n_heads % 2 == 0 and tk % tq == 0 and tk % LANES == 0
    return pl.pallas_call(
        functools.partial(_fox_attn_kernel, tq, tk, dh),
        grid=(B, n_heads // 2, S // tq),
        in_specs=[pl.BlockSpec((1, tq, LANES), lambda b, p, i: (b, i, p)),
                  pl.BlockSpec((1, S, 2 * LANES), lambda b, p, i: (b, 0, p)),
                  pl.BlockSpec((1, S, LANES), lambda b, p, i: (b, 0, p))],
        out_specs=pl.BlockSpec((1, tq, LANES), lambda b, p, i: (b, i, p)),
        out_shape=jax.ShapeDtypeStruct((B, S, AW), BF16),
        scratch_shapes=[pltpu.VMEM((2, tq, LANES), F32), pltpu.VMEM((2, tq, LANES), F32)],
        compiler_params=_cparams("arbitrary", "arbitrary", "arbitrary"),
        name="fox_attn",
    )(q, k, v)


def _merge_kernel(tiles_per_seq, halo,
                  x_ref, a_ref, gb_ref, z_ref, zprev_ref, cw_ref, ga_ref, gv_ref,
                  wua_ref, wuc_ref, wo_ref, o_ref):
    i = pl.program_id(0)
    z = z_ref[...].astype(F32)
    tm = z.shape[0]
    prev = jnp.where(i % tiles_per_seq == 0, 0.0, zprev_ref[...].astype(F32))
    zext = jnp.concatenate([prev, z], axis=0)
    cw = cw_ref[...]
    kk = cw.shape[0]
    conv = z * cw[kk - 1:kk, :]
    for d in range(1, kk):
        conv = conv + pltpu.roll(zext, d, axis=0)[halo:halo + tm] * cw[kk - 1 - d:kk - d, :]
    c = (gb_ref[...].astype(F32) * conv).astype(BF16)
    ya = jnp.dot(a_ref[...], wua_ref[...], preferred_element_type=F32)
    yc = jnp.dot(c, wuc_ref[...], preferred_element_type=F32)
    merged = ga_ref[...].astype(F32) * ya + gv_ref[...].astype(F32) * yc
    o_ref[...] = x_ref[...] + jnp.dot(merged.astype(BF16), wo_ref[...],
                                      preferred_element_type=F32)


def _merge(x2, attn, gb, z, conv_w, ga, gv, w_branch, w_out, seq, tm):
    T, D = x2.shape
    aw = attn.shape[1]
    cw = z.shape[1]
    halo = 16
    assert conv_w.shape[0] - 1 <= halo
    wua = w_branch[0].astype(BF16)
    wuc = w_branch[1].astype(BF16)
    wo = w_out.astype(BF16)

    def full(a):
        return pl.BlockSpec(a.shape, lambda i: (0, 0))

    def tok(width):
        return pl.BlockSpec((tm, width), lambda i: (i, 0))

    hb = tm // halo
    return pl.pallas_call(
        functools.partial(_merge_kernel, seq // tm, halo),
        grid=(T // tm,),
        in_specs=[tok(D), tok(aw), tok(cw), tok(cw),
                  pl.BlockSpec((halo, cw), lambda i: (jnp.maximum(i * hb - 1, 0), 0)),
                  full(conv_w), tok(D), tok(D), full(wua), full(wuc), full(wo)],
        out_specs=tok(D),
        out_shape=jax.ShapeDtypeStruct((T, D), F32),
        compiler_params=_cparams("arbitrary"),
        name="merge",
    )(x2, attn, gb, z, z, conv_w, ga, gv, wua, wuc, wo)


def _top_rows(xs, k, want_rank=False):
    xs = list(xs)
    rows = [[] for _ in xs]
    for r in range(k):
        for i, x in enumerate(xs):
            m = jnp.max(x, axis=0, keepdims=True)
            rows[i].append(m)
            xs[i] = jnp.where(x == m, -REMOVED * (1.0 + r * 2.0 ** -23), x)
    if not want_rank:
        return rows
    ranks = [jnp.where(x <= -REMOVED, (x * (-1.0 / REMOVED) - 1.0) * 2.0 ** 23, float(k))
             for x in xs]
    return list(zip(rows, ranks))


def _ceil_to(n, m):
    return -(-n // m) * m


def _stack_rows(rows, t):
    n = len(rows)
    ridx = lax.broadcasted_iota(jnp.int32, (n, t), 0)
    out = jnp.zeros((n, t), F32)
    for r, row in enumerate(rows):
        out = jnp.where(ridx == r, row, out)
    return out


def _route(s1, s2, topk):
    t = s1.shape[1]
    (v1, rank1), (v2, rank2) = _top_rows([s1, s2], topk, True)
    v1a = _stack_rows(v1, t)
    v2a = _stack_rows(v2, t)
    few = 4
    cands = [v1[a] + v2a[:_ceil_to(topk // (a + 1), SUBLANES)]
             for a in range(topk) if topk // (a + 1) >= few]
    cands += [v2[b] + v1a[:_ceil_to(topk // (b + 1), SUBLANES)] for b in range(few - 1)]
    top, = _top_rows([jnp.concatenate(cands, axis=0)], topk)
    tau = top[topk - 1]
    zsum = jnp.zeros_like(tau)
    for r in range(topk):
        zsum = zsum + jnp.exp(top[r] - top[0])
    n_rows = []
    for a in range(topk):
        nb = _ceil_to(topk // (a + 1), SUBLANES)
        n_rows.append(jnp.sum(jnp.where(v1[a] + v2a[:nb] >= tau, 1.0, 0.0), axis=0,
                              keepdims=True))
    return rank1, rank2, v1[0], v2[0], zsum, n_rows


def _peer_prep_kernel(n_heads, half, topk,
                      x_ref, g_ref, wqt_ref, k1_ref, k2_ref, u_ref, v_ref,
                      ht_ref, rk_ref, e2_ref, c_ref, n_ref, ub_ref, vt_ref):
    ub_ref[...] = u_ref[...].astype(BF16)
    vt_ref[...] = v_ref[...].T.astype(BF16)
    h2 = _rms(x_ref[...], g_ref[...])
    ht = h2.T.astype(BF16)
    ht_ref[...] = ht
    qt = jnp.dot(wqt_ref[...], ht, preferred_element_type=F32)
    tp = ht.shape[1]
    for h in range(n_heads):
        q1 = qt[h * 2 * half:h * 2 * half + half].astype(BF16)
        q2 = qt[h * 2 * half + half:(h + 1) * 2 * half].astype(BF16)
        s1 = jnp.dot(k1_ref[h], q1, preferred_element_type=F32)
        s2 = jnp.dot(k2_ref[h], q2, preferred_element_type=F32)
        rank1, rank2, m1, m2, zsum, n_rows = _route(s1, s2, topk)
        rank1 = rank1.astype(BF16)
        n = jnp.zeros(s1.shape, BF16)
        for a in range(topk):
            n = jnp.where(rank1 == float(a), _row_to_bf16_tile(n_rows[a], s1.shape[0]), n)
        rank2 = rank2.astype(rk_ref.dtype)
        e2 = jnp.exp(s2 - m2).astype(e2_ref.dtype)
        for t in range(tp // TOK_CHUNK):
            rk_ref[h, t] = rank2[:, t * TOK_CHUNK:(t + 1) * TOK_CHUNK]
            e2_ref[h, t] = e2[:, t * TOK_CHUNK:(t + 1) * TOK_CHUNK]
        c_ref[h] = jnp.exp(s1 - m1) / zsum
        n_ref[h] = n.astype(F32)


def _peer_prep(x1, g_ffn, w_peer_q, k1, k2, u, v, tp):
    T, D = x1.shape
    _, PH, QD = w_peer_q.shape
    NK, HALF = k1.shape[1], k1.shape[2]
    NE = u.shape[0]
    assert NK >= PEER_TOPK and PEER_TOPK % 8 == 0
    steps = T // tp
    es = NE // steps
    assert NE % steps == 0 and es % LANES == 0
    wqt = w_peer_q.reshape(D, PH * QD).T.astype(BF16)
    k1b = k1.astype(BF16)
    k2b = k2.astype(BF16)
    big = pl.BlockSpec((PH, NK, tp), lambda i: (0, 0, i))
    tiled = pl.BlockSpec((PH, tp // TOK_CHUNK, NK, TOK_CHUNK), lambda i: (0, i, 0, 0))
    rows = pl.BlockSpec((es, D), lambda i: (i, 0))
    return pl.pallas_call(
        functools.partial(_peer_prep_kernel, PH, HALF, PEER_TOPK),
        grid=(steps,),
        in_specs=[pl.BlockSpec((tp, D), lambda i: (i, 0)),
                  pl.BlockSpec(g_ffn.shape, lambda i: (0, 0)),
                  pl.BlockSpec(wqt.shape, lambda i: (0, 0)),
                  pl.BlockSpec(k1b.shape, lambda i: (0, 0, 0)),
                  pl.BlockSpec(k2b.shape, lambda i: (0, 0, 0)), rows, rows],
        out_specs=[pl.BlockSpec((D, tp), lambda i: (0, i)), tiled, tiled, big, big, rows,
                   pl.BlockSpec((D, es), lambda i: (0, i))],
        out_shape=[jax.ShapeDtypeStruct((D, T), BF16),
                   jax.ShapeDtypeStruct((PH, T // TOK_CHUNK, NK, TOK_CHUNK), BF16),
                   jax.ShapeDtypeStruct((PH, T // TOK_CHUNK, NK, TOK_CHUNK), BF16),
                   jax.ShapeDtypeStruct((PH, NK, T), F32),
                   jax.ShapeDtypeStruct((PH, NK, T), F32),
                   jax.ShapeDtypeStruct((NE, D), BF16),
                   jax.ShapeDtypeStruct((D, NE), BF16)],
        compiler_params=_cparams("arbitrary"),
        name="peer_prep",
    )(x1, g_ffn, wqt, k1b, k2b, u, v)


def _gelu(x):
    return 0.5 * x * (1.0 + lax.erf(x * (1.0 / math.sqrt(2.0))))


BF16_ROWS = 16


def _row_to_bf16_tile(row, n):
    packed = jnp.broadcast_to(row, (BF16_ROWS, row.shape[1])).astype(BF16)
    return jnp.tile(packed, (n // BF16_ROWS, 1))


def _peer_main_kernel(n_heads, nkeys, rows,
                      ht_ref, u_ref, vt_ref, rk_ref, e2_ref, c_ref, n_ref,
                      o_ref, acc_ref, w_ref):
    j = pl.program_id(1)
    tm = ht_ref.shape[1]

    @pl.when(j == 0)
    def _():
        acc_ref[...] = jnp.zeros_like(acc_ref)

    ht = ht_ref[...]
    for r in range(rows):
        act = jnp.dot(u_ref[r * nkeys:(r + 1) * nkeys, :], ht, preferred_element_type=F32)
        for ci in range(tm // TOK_CHUNK):
            tok = slice(ci * TOK_CHUNK, (ci + 1) * TOK_CHUNK)
            g = jnp.zeros((nkeys, TOK_CHUNK), BF16)
            for h in range(n_heads):
                n_b = _row_to_bf16_tile(n_ref[h, r:r + 1, tok], nkeys)
                c_b = _row_to_bf16_tile(c_ref[h, r:r + 1, tok], nkeys)
                g = g + jnp.where(rk_ref[h, ci] < n_b, e2_ref[h, ci] * c_b,
                                  jnp.zeros_like(g))
            w_ref[r * nkeys:(r + 1) * nkeys, tok] = g * _gelu(act[:, tok]).astype(BF16)
    acc_ref[...] += jnp.dot(vt_ref[...], w_ref[...], preferred_element_type=F32)

    @pl.when(j == pl.num_programs(1) - 1)
    def _():
        o_ref[...] = acc_ref[...].T


def _peer_main(ht, ub, vt, rk, e2, c, n, tm):
    D, T = ht.shape
    PH, _, NK, _ = rk.shape
    NE = ub.shape[0]
    rows = min(32, NK)
    eb = rows * NK
    tile = pl.BlockSpec((PH, tm // TOK_CHUNK, NK, TOK_CHUNK), lambda i, j: (0, i, 0, 0))
    rowblk = pl.BlockSpec((PH, rows, tm), lambda i, j: (0, j, i))
    return pl.pallas_call(
        functools.partial(_peer_main_kernel, PH, NK, rows),
        grid=(T // tm, NE // eb),
        in_specs=[pl.BlockSpec((D, tm), lambda i, j: (0, i)),
                  pl.BlockSpec((eb, D), lambda i, j: (j, 0)),
                  pl.BlockSpec((D, eb), lambda i, j: (0, j)),
                  tile, tile, rowblk, rowblk],
        out_specs=pl.BlockSpec((tm, D), lambda i, j: (i, 0)),
        out_shape=jax.ShapeDtypeStruct((T, D), F32),
        scratch_shapes=[pltpu.VMEM((D, tm), F32), pltpu.VMEM((eb, tm), BF16)],
        compiler_params=_cparams("arbitrary", "arbitrary"),
        name="peer_main",
    )(ht, ub, vt, rk, e2, c, n)


def _ple_out_kernel(final_norm, x_ref, f_ref, p_ref, gp_ref, wg_ref, wp_ref, gf_ref, o_ref):
    x = x_ref[...] + f_ref[...]
    h = _rms(x, gp_ref[...]).astype(BF16)
    gate = jax.nn.sigmoid(jnp.dot(h, wg_ref[...], preferred_element_type=F32))
    proj = jnp.dot(p_ref[...].astype(BF16), wp_ref[...], preferred_element_type=F32)
    y = x + gate * proj
    o_ref[...] = _rms(y, gf_ref[...]) if final_norm else y


def _ple_out(x1, ffn, p2, g_ple, w_gate, w_proj, g_final, final_norm, tm):
    T, D = x1.shape
    wg = w_gate.astype(BF16)
    wp = w_proj.astype(BF16)

    def full(a):
        return pl.BlockSpec(a.shape, lambda i: (0, 0))

    return pl.pallas_call(
        functools.partial(_ple_out_kernel, final_norm),
        grid=(T // tm,),
        in_specs=[pl.BlockSpec((tm, D), lambda i: (i, 0)),
                  pl.BlockSpec((tm, D), lambda i: (i, 0)),
                  pl.BlockSpec((tm, p2.shape[1]), lambda i: (i, 0)),
                  full(g_ple), full(wg), full(wp), full(g_final)],
        out_specs=pl.BlockSpec((tm, D), lambda i: (i, 0)),
        out_shape=jax.ShapeDtypeStruct((T, D), F32),
        compiler_params=_cparams("arbitrary"),
        name="ple_out",
    )(x1, ffn, p2, g_ple, wg, wp, g_final)


def _tile(n, want):
    t = min(n, want)
    assert n % t == 0
    return t


def kernel(x, p, g_mix, w_in, b_f, conv_w, w_branch, w_out, g_ffn, w_peer_q, peer_k1,
           peer_k2, peer_u, peer_v, g_ple, w_ple_gate, w_ple_proj, g_final):
    B, S, D = x.shape
    T = B * S
    depth = w_in.shape[0]
    n_heads = b_f.shape[1]
    aw = w_branch.shape[2]
    cw = conv_w.shape[2]
    tm = _tile(S, 512)
    xt = x.reshape(T, D)
    for i in range(depth):
        q, k, v, gb, z, ga, gv = _in_proj(
            xt, g_mix[i].reshape(1, D), w_in[i], b_f[i], (aw, n_heads, cw), S, tm)
        attn = _fox_attn(q.reshape(B, S, aw), k.reshape(B, S, -1), v.reshape(B, S, aw),
                         n_heads, _tile(S, 512), _tile(S, 2048))
        x1 = _merge(xt, attn.reshape(T, aw), gb, z, conv_w[i], ga, gv, w_branch[i], w_out[i],
                    S, tm)
        ht, rk, e2, cc, nn, ub, vt = _peer_prep(x1, g_ffn[i].reshape(1, D), w_peer_q[i],
                                                peer_k1[i], peer_k2[i], peer_u[i], peer_v[i],
                                                _tile(T, 256))
        ffn = _peer_main(ht, ub, vt, rk, e2, cc, nn, _tile(T, 512))
        xt = _ple_out(x1, ffn, p[i].reshape(T, p.shape[-1]), g_ple[i].reshape(1, D),
                      w_ple_gate[i], w_ple_proj[i], g_final.reshape(1, D), i == depth - 1, tm)
    return xt.reshape(B, S, D)
```

```python
import functools
import math

import jax
import jax.numpy as jnp
import numpy as np
from jax import lax
from jax.experimental import pallas as pl
from jax.experimental.pallas import tpu as pltpu

EPS = 1e-6
PEER_TOPK = 16
LANES = 128
SUBLANES = 8
NEG_BIG = -1e30
REMOVED = 2.0 ** 100
TOK_CHUNK = 256
VMEM_LIMIT = 56 * 1024 * 1024

F32 = jnp.float32
BF16 = jnp.bfloat16


def _cparams(*sem, flags=None):
    return pltpu.CompilerParams(dimension_semantics=sem, vmem_limit_bytes=VMEM_LIMIT,
                                flags=flags)


def _rms(x, g):
    return x * lax.rsqrt(jnp.mean(x * x, axis=-1, keepdims=True) + EPS) * g


def _log_sigmoid(x):
    return jnp.minimum(x, 0.0) - jnp.log1p(jnp.exp(-jnp.abs(x)))


def _split3(x):
    hi = x.astype(BF16)
    r = x - hi.astype(F32)
    mid = r.astype(BF16)
    lo = (r - mid.astype(F32)).astype(BF16)
    return hi, mid, lo


def _in_proj_kernel(tiles_per_seq, n_heads, scale,
                    x_ref, g_ref, wq_ref, wk_ref, wv_ref, wf_ref, bf_ref, wgb_ref, wgc_ref,
                    wxc_ref, wga_ref, wgv_ref, tri_ref, place_ref,
                    q_ref, k_ref, v_ref, gb_ref, z_ref, ga_ref, gv_ref, carry_ref):
    i = pl.program_id(0)
    h = _rms(x_ref[...], g_ref[...]).astype(BF16)

    def mm(w_ref):
        return jnp.dot(h, w_ref[...], preferred_element_type=F32)

    q_ref[...] = (mm(wq_ref) * scale).astype(BF16)
    v_ref[...] = mm(wv_ref).astype(BF16)
    gb_ref[...] = mm(wgb_ref).astype(BF16)
    z_ref[...] = (mm(wgc_ref) * mm(wxc_ref)).astype(BF16)
    ga_ref[...] = jax.nn.sigmoid(mm(wga_ref)).astype(BF16)
    gv_ref[...] = jax.nn.sigmoid(mm(wgv_ref)).astype(BF16)

    logf = _log_sigmoid(mm(wf_ref) + bf_ref[...])
    lane = lax.broadcasted_iota(jnp.int32, logf.shape, 1)
    logf = jnp.where(lane < n_heads, logf, 0.0)
    tri = tri_ref[...]
    hi, mid, lo = _split3(logf)
    cs = (jnp.dot(tri, hi, preferred_element_type=F32)
          + jnp.dot(tri, mid, preferred_element_type=F32)
          + jnp.dot(tri, lo, preferred_element_type=F32))

    @pl.when(i % tiles_per_seq == 0)
    def _():
        carry_ref[...] = jnp.zeros_like(carry_ref)

    c = cs + carry_ref[...]
    tm = c.shape[0]
    carry_ref[...] = c[tm - 1:tm, :]
    pieces = jnp.concatenate(_split3(-c), axis=1)
    feats = mm(wk_ref).astype(BF16)
    bias = jnp.dot(pieces, place_ref[...], preferred_element_type=F32).astype(BF16)
    for pair in range(n_heads // 2):
        src = slice(pair * LANES, (pair + 1) * LANES)
        k_ref[:, 2 * pair * LANES:(2 * pair + 1) * LANES] = feats[:, src]
        k_ref[:, (2 * pair + 1) * LANES:(2 * pair + 2) * LANES] = bias[:, src]


def _in_proj(x2, g_mix, w_in, b_f, sizes, seq, tm):
    T, D = x2.shape
    aw, n_heads, cw = sizes
    offs = [0, aw, 2 * aw, 3 * aw, 3 * aw + n_heads, 3 * aw + n_heads + cw,
            3 * aw + n_heads + 2 * cw, 3 * aw + n_heads + 3 * cw,
            3 * aw + n_heads + 3 * cw + D, 3 * aw + n_heads + 3 * cw + 2 * D]
    seg = [w_in[:, offs[n]:offs[n + 1]] for n in range(9)]
    wq, wk, wv, wf, wgb, wgc, wxc, wga, wgv = seg
    wf = jnp.pad(wf, ((0, 0), (0, LANES - n_heads)))
    bf = jnp.pad(b_f.reshape(1, n_heads), ((0, 0), (0, LANES - n_heads)))
    dh = aw // n_heads
    assert 2 * dh == LANES
    kw = n_heads * LANES
    place = np.zeros((3 * LANES, kw // 2), np.float32)
    for hd in range(n_heads):
        for s in range(3):
            place[s * LANES + hd, (hd // 2) * LANES + 3 * (hd % 2) + s] = 1.0
    place = jnp.asarray(place, BF16)
    ws = [w.astype(BF16) for w in (wq, wk, wv, wf)] + [bf] + \
         [w.astype(BF16) for w in (wgb, wgc, wxc, wga, wgv)]
    tri = (lax.broadcasted_iota(jnp.int32, (tm, tm), 0)
           >= lax.broadcasted_iota(jnp.int32, (tm, tm), 1)).astype(BF16)
    scale = dh ** -0.5

    def full(a):
        return pl.BlockSpec(a.shape, lambda i: (0, 0))

    def tok(width):
        return pl.BlockSpec((tm, width), lambda i: (i, 0))

    outs = [(aw, BF16), (kw, BF16), (aw, BF16), (cw, BF16), (cw, BF16), (D, BF16), (D, BF16)]
    return pl.pallas_call(
        functools.partial(_in_proj_kernel, seq // tm, n_heads, scale),
        grid=(T // tm,),
        in_specs=[tok(D), full(g_mix)] + [full(w) for w in ws] + [full(tri), full(place)],
        out_specs=[tok(w) for w, _ in outs],
        out_shape=[jax.ShapeDtypeStruct((T, w), dt) for w, dt in outs],
        scratch_shapes=[pltpu.VMEM((1, LANES), F32)],
        compiler_params=_cparams("arbitrary"),
        name="in_proj",
    )(x2, g_mix, *ws, tri, place)


def _fox_attn_kernel(tq, tk, dh, q_ref, k_ref, v_ref, o_ref, m_ref, acc_ref):
    qi = pl.program_id(2)
    qp = q_ref[0]
    lane = lax.broadcasted_iota(jnp.int32, qp.shape, 1)
    zero = jnp.zeros_like(qp)
    q_aug = []
    for hh in range(2):
        feat = jnp.where(lane < dh, qp, zero) if hh == 0 else jnp.where(lane >= dh, qp, zero)
        pick = jnp.where(lane < 3 * hh, 0.0, jnp.where(lane < 3 * hh + 3, 1.0, 0.0)).astype(BF16)
        q_aug.append(jnp.concatenate([feat, pick], axis=1))

    m_ref[...] = jnp.full(m_ref.shape, NEG_BIG, F32)
    acc_ref[...] = jnp.zeros(acc_ref.shape, F32)

    def step(k0, width, lead):
        kb = k_ref[0, pl.ds(k0, width), :]
        vb = v_ref[0, pl.ds(k0, width), :]
        vlane = lax.broadcasted_iota(jnp.int32, vb.shape, 1)
        vone = jnp.ones_like(vb)
        v_aug = (jnp.where(vlane < dh, vb, vone), jnp.where(vlane >= dh, vb, vone))
        for hh in range(2):
            s = lax.dot_general(q_aug[hh], kb, (((1,), (1,)), ((), ())),
                                preferred_element_type=F32)
            if lead is not None:
                row = lax.broadcasted_iota(jnp.int32, (tq, width - lead), 0)
                col = lax.broadcasted_iota(jnp.int32, (tq, width - lead), 1)
                tail = jnp.where(col <= row, s[:, lead:], NEG_BIG)
                s = tail if lead == 0 else jnp.concatenate([s[:, :lead], tail], axis=1)
            m_prev = m_ref[hh]
            m_new = jnp.maximum(m_prev, jnp.max(s, axis=1, keepdims=True))
            alpha = jnp.exp(m_prev - m_new)
            p = jnp.exp(s - jnp.tile(m_new, (1, width // LANES)))
            acc_ref[hh] = alpha * acc_ref[hh] + jnp.dot(p.astype(BF16), v_aug[hh],
                                                        preferred_element_type=F32)
            m_ref[hh] = m_new

    def body(j, carry):
        step(pl.multiple_of(j * tk, tk), tk, None)
        return carry

    q0 = qi * tq
    n_wide = q0 // tk
    lax.fori_loop(0, n_wide, body, 0)
    k_last = pl.multiple_of(n_wide * tk, tk)
    for m in range(tk // tq):
        @pl.when(q0 - n_wide * tk == m * tq)
        def _():
            step(k_last, (m + 1) * tq, m * tq)

    a0 = acc_ref[0]
    a1 = acc_ref[1]
    out = jnp.where(lane < dh, a0 / pltpu.roll(a0, dh, axis=1), a1 / pltpu.roll(a1, dh, axis=1))
    o_ref[0] = out.astype(o_ref.dtype)


def _fox_attn(q, k, v, n_heads, tq, tk):
    B, S, AW = q.shape
    dh = AW // n_heads
    assert 2 * dh == LANES and n_heads % 2 == 0 and tk % tq == 0 and tk % LANES == 0
    return pl.pallas_call(
        functools.partial(_fox_attn_kernel, tq, tk, dh),
        grid=(B, n_heads // 2, S // tq),
        in_specs=[pl.BlockSpec((1, tq, LANES), lambda b, p, i: (b, i, p)),
                  pl.BlockSpec((1, S, 2 * LANES), lambda b, p, i: (b, 0, p)),
                  pl.BlockSpec((1, S, LANES), lambda b, p, i: (b, 0, p))],
        out_specs=pl.BlockSpec((1, tq, LANES), lambda b, p, i: (b, i, p)),
        out_shape=jax.ShapeDtypeStruct((B, S, AW), BF16),
        scratch_shapes=[pltpu.VMEM((2, tq, LANES), F32), pltpu.VMEM((2, tq, LANES), F32)],
        compiler_params=_cparams("arbitrary", "arbitrary", "arbitrary"),
        name="fox_attn",
    )(q, k, v)


def _merge_kernel(tiles_per_seq, halo,
                  x_ref, a_ref, gb_ref, z_ref, zprev_ref, cw_ref, ga_ref, gv_ref,
                  wua_ref, wuc_ref, wo_ref, o_ref):
    i = pl.program_id(0)
    z = z_ref[...].astype(F32)
    tm = z.shape[0]
    prev = jnp.where(i % tiles_per_seq == 0, 0.0, zprev_ref[...].astype(F32))
    zext = jnp.concatenate([prev, z], axis=0)
    cw = cw_ref[...]
    kk = cw.shape[0]
    conv = z * cw[kk - 1:kk, :]
    for d in range(1, kk):
        conv = conv + pltpu.roll(zext, d, axis=0)[halo:halo + tm] * cw[kk - 1 - d:kk - d, :]
    c = (gb_ref[...].astype(F32) * conv).astype(BF16)
    ya = jnp.dot(a_ref[...], wua_ref[...], preferred_element_type=F32)
    yc = jnp.dot(c, wuc_ref[...], preferred_element_type=F32)
    merged = ga_ref[...].astype(F32) * ya + gv_ref[...].astype(F32) * yc
    o_ref[...] = x_ref[...] + jnp.dot(merged.astype(BF16), wo_ref[...],
                                      preferred_element_type=F32)


def _merge(x2, attn, gb, z, conv_w, ga, gv, w_branch, w_out, seq, tm):
    T, D = x2.shape
    aw = attn.shape[1]
    cw = z.shape[1]
    halo = 16
    assert conv_w.shape[0] - 1 <= halo
    wua = w_branch[0].astype(BF16)
    wuc = w_branch[1].astype(BF16)
    wo = w_out.astype(BF16)

    def full(a):
        return pl.BlockSpec(a.shape, lambda i: (0, 0))

    def tok(width):
        return pl.BlockSpec((tm, width), lambda i: (i, 0))

    hb = tm // halo
    return pl.pallas_call(
        functools.partial(_merge_kernel, seq // tm, halo),
        grid=(T // tm,),
        in_specs=[tok(D), tok(aw), tok(cw), tok(cw),
                  pl.BlockSpec((halo, cw), lambda i: (jnp.maximum(i * hb - 1, 0), 0)),
                  full(conv_w), tok(D), tok(D), full(wua), full(wuc), full(wo)],
        out_specs=tok(D),
        out_shape=jax.ShapeDtypeStruct((T, D), F32),
        compiler_params=_cparams("arbitrary"),
        name="merge",
    )(x2, attn, gb, z, z, conv_w, ga, gv, wua, wuc, wo)


def _top_rows(xs, k, want_rank=False):
    xs = list(xs)
    rows = [[] for _ in xs]
    for r in range(k):
        for i, x in enumerate(xs):
            m = jnp.max(x, axis=0, keepdims=True)
            rows[i].append(m)
            xs[i] = jnp.where(x == m, -REMOVED * (1.0 + r * 2.0 ** -23), x)
    if not want_rank:
        return rows
    ranks = [jnp.where(x <= -REMOVED, (x * (-1.0 / REMOVED) - 1.0) * 2.0 ** 23, float(k))
             for x in xs]
    return list(zip(rows, ranks))


def _ceil_to(n, m):
    return -(-n // m) * m


def _stack_rows(rows, t):
    n = len(rows)
    ridx = lax.broadcasted_iota(jnp.int32, (n, t), 0)
    out = jnp.zeros((n, t), F32)
    for r, row in enumerate(rows):
        out = jnp.where(ridx == r, row, out)
    return out


def _route(s1, s2, topk):
    t = s1.shape[1]
    (v1, rank1), (v2, rank2) = _top_rows([s1, s2], topk, True)
    v1a = _stack_rows(v1, t)
    v2a = _stack_rows(v2, t)
    few = 4
    cands = [v1[a] + v2a[:_ceil_to(topk // (a + 1), SUBLANES)]
             for a in range(topk) if topk // (a + 1) >= few]
    cands += [v2[b] + v1a[:_ceil_to(topk // (b + 1), SUBLANES)] for b in range(few - 1)]
    top, = _top_rows([jnp.concatenate(cands, axis=0)], topk)
    tau = top[topk - 1]
    zsum = jnp.zeros_like(tau)
    for r in range(topk):
        zsum = zsum + jnp.exp(top[r] - top[0])
    n_rows = []
    for a in range(topk):
        nb = _ceil_to(topk // (a + 1), SUBLANES)
        n_rows.append(jnp.sum(jnp.where(v1[a] + v2a[:nb] >= tau, 1.0, 0.0), axis=0,
                              keepdims=True))
    return rank1, rank2, v1[0], v2[0], zsum, n_rows


def _peer_prep_kernel(n_heads, half, topk,
                      x_ref, g_ref, wqt_ref, k1_ref, k2_ref, u_ref, v_ref,
                      ht_ref, rk_ref, e2_ref, c_ref, n_ref, ub_ref, vt_ref):
    ub_ref[...] = u_ref[...].astype(BF16)
    vt_ref[...] = v_ref[...].T.astype(BF16)
    h2 = _rms(x_ref[...], g_ref[...])
    ht = h2.T.astype(BF16)
    ht_ref[...] = ht
    qt = jnp.dot(wqt_ref[...], ht, preferred_element_type=F32)
    tp = ht.shape[1]
    for h in range(n_heads):
        q1 = qt[h * 2 * half:h * 2 * half + half].astype(BF16)
        q2 = qt[h * 2 * half + half:(h + 1) * 2 * half].astype(BF16)
        s1 = jnp.dot(k1_ref[h], q1, preferred_element_type=F32)
        s2 = jnp.dot(k2_ref[h], q2, preferred_element_type=F32)
        rank1, rank2, m1, m2, zsum, n_rows = _route(s1, s2, topk)
        rank1 = rank1.astype(BF16)
        n = jnp.zeros(s1.shape, BF16)
        for a in range(topk):
            n = jnp.where(rank1 == float(a), _row_to_bf16_tile(n_rows[a], s1.shape[0]), n)
        rank2 = rank2.astype(rk_ref.dtype)
        e2 = jnp.exp(s2 - m2).astype(e2_ref.dtype)
        for t in range(tp // TOK_CHUNK):
            rk_ref[h, t] = rank2[:, t * TOK_CHUNK:(t + 1) * TOK_CHUNK]
            e2_ref[h, t] = e2[:, t * TOK_CHUNK:(t + 1) * TOK_CHUNK]
        c_ref[h] = jnp.exp(s1 - m1) / zsum
        n_ref[h] = n.astype(F32)


def _peer_prep(x1, g_ffn, w_peer_q, k1, k2, u, v, tp):
    T, D = x1.shape
    _, PH, QD = w_peer_q.shape
    NK, HALF = k1.shape[1], k1.shape[2]
    NE = u.shape[0]
    assert NK >= PEER_TOPK and PEER_TOPK % 8 == 0
    steps = T // tp
    es = NE // steps
    assert NE % steps == 0 and es % LANES == 0
    wqt = w_peer_q.reshape(D, PH * QD).T.astype(BF16)
    k1b = k1.astype(BF16)
    k2b = k2.astype(BF16)
    big = pl.BlockSpec((PH, NK, tp), lambda i: (0, 0, i))
    tiled = pl.BlockSpec((PH, tp // TOK_CHUNK, NK, TOK_CHUNK), lambda i: (0, i, 0, 0))
    rows = pl.BlockSpec((es, D), lambda i: (i, 0))
    return pl.pallas_call(
        functools.partial(_peer_prep_kernel, PH, HALF, PEER_TOPK),
        grid=(steps,),
        in_specs=[pl.BlockSpec((tp, D), lambda i: (i, 0)),
                  pl.BlockSpec(g_ffn.shape, lambda i: (0, 0)),
                  pl.BlockSpec(wqt.shape, lambda i: (0, 0)),
                  pl.BlockSpec(k1b.shape, lambda i: (0, 0, 0)),
                  pl.BlockSpec(k2b.shape, lambda i: (0, 0, 0)), rows, rows],
        out_specs=[pl.BlockSpec((D, tp), lambda i: (0, i)), tiled, tiled, big, big, rows,
                   pl.BlockSpec((D, es), lambda i: (0, i))],
        out_shape=[jax.ShapeDtypeStruct((D, T), BF16),
                   jax.ShapeDtypeStruct((PH, T // TOK_CHUNK, NK, TOK_CHUNK), BF16),
                   jax.ShapeDtypeStruct((PH, T // TOK_CHUNK, NK, TOK_CHUNK), BF16),
                   jax.ShapeDtypeStruct((PH, NK, T), F32),
                   jax.ShapeDtypeStruct((PH, NK, T), F32),
                   jax.ShapeDtypeStruct((NE, D), BF16),
                   jax.ShapeDtypeStruct((D, NE), BF16)],
        compiler_params=_cparams("arbitrary"),
        name="peer_prep",
    )(x1, g_ffn, wqt, k1b, k2b, u, v)


def _gelu(x):
    return 0.5 * x * (1.0 + lax.erf(x * (1.0 / math.sqrt(2.0))))


BF16_ROWS = 16


def _row_to_bf16_tile(row, n):
    packed = jnp.broadcast_to(row, (BF16_ROWS, row.shape[1])).astype(BF16)
    return jnp.tile(packed, (n // BF16_ROWS, 1))


def _peer_main_kernel(n_heads, nkeys, rows,
                      ht_ref, u_ref, vt_ref, rk_ref, e2_ref, c_ref, n_ref,
                      o_ref, acc_ref, w_ref):
    j = pl.program_id(1)
    tm = ht_ref.shape[1]

    @pl.when(j == 0)
    def _():
        acc_ref[...] = jnp.zeros_like(acc_ref)

    ht = ht_ref[...]
    for r in range(rows):
        act = jnp.dot(u_ref[r * nkeys:(r + 1) * nkeys, :], ht, preferred_element_type=F32)
        for ci in range(tm // TOK_CHUNK):
            tok = slice(ci * TOK_CHUNK, (ci + 1) * TOK_CHUNK)
            g = jnp.zeros((nkeys, TOK_CHUNK), BF16)
            for h in range(n_heads):
                n_b = _row_to_bf16_tile(n_ref[h, r:r + 1, tok], nkeys)
                c_b = _row_to_bf16_tile(c_ref[h, r:r + 1, tok], nkeys)
                g = g + jnp.where(rk_ref[h, ci] < n_b, e2_ref[h, ci] * c_b,
                                  jnp.zeros_like(g))
            w_ref[r * nkeys:(r + 1) * nkeys, tok] = g * _gelu(act[:, tok]).astype(BF16)
    acc_ref[...] += jnp.dot(vt_ref[...], w_ref[...], preferred_element_type=F32)

    @pl.when(j == pl.num_programs(1) - 1)
    def _():
        o_ref[...] = acc_ref[...].T.astype(o_ref.dtype)


def _peer_main(ht, ub, vt, rk, e2, c, n, tm):
    D, T = ht.shape
    PH, _, NK, _ = rk.shape
    NE = ub.shape[0]
    rows = min(32, NK)
    eb = rows * NK
    tile = pl.BlockSpec((PH, tm // TOK_CHUNK, NK, TOK_CHUNK), lambda i, j: (0, i, 0, 0))
    rowblk = pl.BlockSpec((PH, rows, tm), lambda i, j: (0, j, i))
    return pl.pallas_call(
        functools.partial(_peer_main_kernel, PH, NK, rows),
        grid=(T // tm, NE // eb),
        in_specs=[pl.BlockSpec((D, tm), lambda i, j: (0, i)),
                  pl.BlockSpec((eb, D), lambda i, j: (j, 0)),
                  pl.BlockSpec((D, eb), lambda i, j: (0, j)),
                  tile, tile, rowblk, rowblk],
        out_specs=pl.BlockSpec((tm, D), lambda i, j: (i, 0)),
        out_shape=jax.ShapeDtypeStruct((T, D), BF16),
        scratch_shapes=[pltpu.VMEM((D, tm), F32), pltpu.VMEM((eb, tm), BF16)],
        compiler_params=_cparams("arbitrary", "arbitrary"),
        name="peer_main",
    )(ht, ub, vt, rk, e2, c, n)


def _ple_out_kernel(final_norm, x_ref, f_ref, p_ref, gp_ref, wg_ref, wp_ref, gf_ref, o_ref):
    x = x_ref[...] + f_ref[...].astype(F32)
    h = _rms(x, gp_ref[...]).astype(BF16)
    gate = jax.nn.sigmoid(jnp.dot(h, wg_ref[...], preferred_element_type=F32))
    proj = jnp.dot(p_ref[...].astype(BF16), wp_ref[...], preferred_element_type=F32)
    y = x + gate * proj
    o_ref[...] = _rms(y, gf_ref[...]) if final_norm else y


def _ple_out(x1, ffn, p2, g_ple, w_gate, w_proj, g_final, final_norm, tm):
    T, D = x1.shape
    wg = w_gate.astype(BF16)
    wp = w_proj.astype(BF16)

    def full(a):
        return pl.BlockSpec(a.shape, lambda i: (0, 0))

    return pl.pallas_call(
        functools.partial(_ple_out_kernel, final_norm),
        grid=(T // tm,),
        in_specs=[pl.BlockSpec((tm, D), lambda i: (i, 0)),
                  pl.BlockSpec((tm, D), lambda i: (i, 0)),
                  pl.BlockSpec((tm, p2.shape[1]), lambda i: (i, 0)),
                  full(g_ple), full(wg), full(wp), full(g_final)],
        out_specs=pl.BlockSpec((tm, D), lambda i: (i, 0)),
        out_shape=jax.ShapeDtypeStruct((T, D), F32),
        compiler_params=_cparams("arbitrary"),
        name="ple_out",
    )(x1, ffn, p2, g_ple, wg, wp, g_final)


def _tile(n, want):
    t = min(n, want)
    assert n % t == 0
    return t


def kernel(x, p, g_mix, w_in, b_f, conv_w, w_branch, w_out, g_ffn, w_peer_q, peer_k1,
           peer_k2, peer_u, peer_v, g_ple, w_ple_gate, w_ple_proj, g_final):
    B, S, D = x.shape
    T = B * S
    depth = w_in.shape[0]
    n_heads = b_f.shape[1]
    aw = w_branch.shape[2]
    cw = conv_w.shape[2]
    tm = _tile(S, 512)
    xt = x.reshape(T, D)
    for i in range(depth):
        q, k, v, gb, z, ga, gv = _in_proj(
            xt, g_mix[i].reshape(1, D), w_in[i], b_f[i], (aw, n_heads, cw), S, tm)
        attn = _fox_attn(q.reshape(B, S, aw), k.reshape(B, S, -1), v.reshape(B, S, aw),
                         n_heads, _tile(S, 512), _tile(S, 2048))
        x1 = _merge(xt, attn.reshape(T, aw), gb, z, conv_w[i], ga, gv, w_branch[i], w_out[i],
                    S, _tile(S, 1024))
        ht, rk, e2, cc, nn, ub, vt = _peer_prep(x1, g_ffn[i].reshape(1, D), w_peer_q[i],
                                                peer_k1[i], peer_k2[i], peer_u[i], peer_v[i],
                                                _tile(T, 256))
        ffn = _peer_main(ht, ub, vt, rk, e2, cc, nn, _tile(T, 512))
        xt = _ple_out(x1, ffn, p[i].reshape(T, p.shape[-1]), g_ple[i].reshape(1, D),
                      w_ple_gate[i], w_ple_proj[i], g_final.reshape(1, D), i == depth - 1,
                      _tile(S, 1024))
    return xt.reshape(B, S, D)
```

```python
import functools
import math

import jax
import jax.numpy as jnp
import numpy as np
from jax import lax
from jax.experimental import pallas as pl
from jax.experimental.pallas import tpu as pltpu

EPS = 1e-6
PEER_TOPK = 16
LANES = 128
SUBLANES = 8
NEG_BIG = -1e30
REMOVED = 2.0 ** 100
TOK_CHUNK = 256
VMEM_LIMIT = 56 * 1024 * 1024

F32 = jnp.float32
BF16 = jnp.bfloat16


def _cparams(*sem, flags=None):
    return pltpu.CompilerParams(dimension_semantics=sem, vmem_limit_bytes=VMEM_LIMIT,
                                flags=flags)


def _rms(x, g):
    return x * lax.rsqrt(jnp.mean(x * x, axis=-1, keepdims=True) + EPS) * g


def _log_sigmoid(x):
    return jnp.minimum(x, 0.0) - jnp.log1p(jnp.exp(-jnp.abs(x)))


def _split3(x):
    hi = x.astype(BF16)
    r = x - hi.astype(F32)
    mid = r.astype(BF16)
    lo = (r - mid.astype(F32)).astype(BF16)
    return hi, mid, lo


def _in_proj_kernel(tiles_per_seq, n_heads, scale,
                    x_ref, g_ref, wq_ref, wk_ref, wv_ref, wf_ref, bf_ref, wgb_ref, wgc_ref,
                    wxc_ref, wga_ref, wgv_ref, tri_ref, place_ref,
                    q_ref, k_ref, v_ref, gb_ref, z_ref, ga_ref, gv_ref, carry_ref):
    i = pl.program_id(0)
    h = _rms(x_ref[...], g_ref[...]).astype(BF16)

    def mm(w_ref):
        return jnp.dot(h, w_ref[...], preferred_element_type=F32)

    q_ref[...] = (mm(wq_ref) * scale).astype(BF16)
    v_ref[...] = mm(wv_ref).astype(BF16)
    gb_ref[...] = mm(wgb_ref).astype(BF16)
    z_ref[...] = (mm(wgc_ref) * mm(wxc_ref)).astype(BF16)
    ga_ref[...] = jax.nn.sigmoid(mm(wga_ref)).astype(BF16)
    gv_ref[...] = jax.nn.sigmoid(mm(wgv_ref)).astype(BF16)

    logf = _log_sigmoid(mm(wf_ref) + bf_ref[...])
    lane = lax.broadcasted_iota(jnp.int32, logf.shape, 1)
    logf = jnp.where(lane < n_heads, logf, 0.0)
    tri = tri_ref[...]
    hi, mid, lo = _split3(logf)
    cs = (jnp.dot(tri, hi, preferred_element_type=F32)
          + jnp.dot(tri, mid, preferred_element_type=F32)
          + jnp.dot(tri, lo, preferred_element_type=F32))

    @pl.when(i % tiles_per_seq == 0)
    def _():
        carry_ref[...] = jnp.zeros_like(carry_ref)

    c = cs + carry_ref[...]
    tm = c.shape[0]
    carry_ref[...] = c[tm - 1:tm, :]
    pieces = jnp.concatenate(_split3(-c), axis=1)
    feats = mm(wk_ref).astype(BF16)
    bias = jnp.dot(pieces, place_ref[...], preferred_element_type=F32).astype(BF16)
    for pair in range(n_heads // 2):
        src = slice(pair * LANES, (pair + 1) * LANES)
        k_ref[:, 2 * pair * LANES:(2 * pair + 1) * LANES] = feats[:, src]
        k_ref[:, (2 * pair + 1) * LANES:(2 * pair + 2) * LANES] = bias[:, src]


def _in_proj(x2, g_mix, w_in, b_f, sizes, seq, tm):
    T, D = x2.shape
    aw, n_heads, cw = sizes
    offs = [0, aw, 2 * aw, 3 * aw, 3 * aw + n_heads, 3 * aw + n_heads + cw,
            3 * aw + n_heads + 2 * cw, 3 * aw + n_heads + 3 * cw,
            3 * aw + n_heads + 3 * cw + D, 3 * aw + n_heads + 3 * cw + 2 * D]
    seg = [w_in[:, offs[n]:offs[n + 1]] for n in range(9)]
    wq, wk, wv, wf, wgb, wgc, wxc, wga, wgv = seg
    wf = jnp.pad(wf, ((0, 0), (0, LANES - n_heads)))
    bf = jnp.pad(b_f.reshape(1, n_heads), ((0, 0), (0, LANES - n_heads)))
    dh = aw // n_heads
    assert 2 * dh == LANES
    kw = n_heads * LANES
    place = np.zeros((3 * LANES, kw // 2), np.float32)
    for hd in range(n_heads):
        for s in range(3):
            place[s * LANES + hd, (hd // 2) * LANES + 3 * (hd % 2) + s] = 1.0
    place = jnp.asarray(place, BF16)
    ws = [w.astype(BF16) for w in (wq, wk, wv, wf)] + [bf] + \
         [w.astype(BF16) for w in (wgb, wgc, wxc, wga, wgv)]
    tri = (lax.broadcasted_iota(jnp.int32, (tm, tm), 0)
           >= lax.broadcasted_iota(jnp.int32, (tm, tm), 1)).astype(BF16)
    scale = dh ** -0.5

    def full(a):
        return pl.BlockSpec(a.shape, lambda i: (0, 0))

    def tok(width):
        return pl.BlockSpec((tm, width), lambda i: (i, 0))

    outs = [(aw, BF16), (kw, BF16), (aw, BF16), (cw, BF16), (cw, BF16), (D, BF16), (D, BF16)]
    return pl.pallas_call(
        functools.partial(_in_proj_kernel, seq // tm, n_heads, scale),
        grid=(T // tm,),
        in_specs=[tok(D), full(g_mix)] + [full(w) for w in ws] + [full(tri), full(place)],
        out_specs=[tok(w) for w, _ in outs],
        out_shape=[jax.ShapeDtypeStruct((T, w), dt) for w, dt in outs],
        scratch_shapes=[pltpu.VMEM((1, LANES), F32)],
        compiler_params=_cparams("arbitrary"),
        name="in_proj",
    )(x2, g_mix, *ws, tri, place)


def _fox_attn_kernel(tq, tk, dh, q_ref, k_ref, v_ref, o_ref, m_ref, acc_ref):
    qi = pl.program_id(2)
    qp = q_ref[0]
    lane = lax.broadcasted_iota(jnp.int32, qp.shape, 1)
    zero = jnp.zeros_like(qp)
    q_aug = []
    for hh in range(2):
        feat = jnp.where(lane < dh, qp, zero) if hh == 0 else jnp.where(lane >= dh, qp, zero)
        pick = jnp.where(lane < 3 * hh, 0.0, jnp.where(lane < 3 * hh + 3, 1.0, 0.0)).astype(BF16)
        q_aug.append(jnp.concatenate([feat, pick], axis=1))

    m_ref[...] = jnp.full(m_ref.shape, NEG_BIG, F32)
    acc_ref[...] = jnp.zeros(acc_ref.shape, F32)

    def step(k0, width, lead):
        kb = k_ref[0, pl.ds(k0, width), :]
        vb = v_ref[0, pl.ds(k0, width), :]
        vlane = lax.broadcasted_iota(jnp.int32, vb.shape, 1)
        vone = jnp.ones_like(vb)
        v_aug = (jnp.where(vlane < dh, vb, vone), jnp.where(vlane >= dh, vb, vone))
        for hh in range(2):
            s = lax.dot_general(q_aug[hh], kb, (((1,), (1,)), ((), ())),
                                preferred_element_type=F32)
            if lead is not None:
                row = lax.broadcasted_iota(jnp.int32, (tq, width - lead), 0)
                col = lax.broadcasted_iota(jnp.int32, (tq, width - lead), 1)
                tail = jnp.where(col <= row, s[:, lead:], NEG_BIG)
                s = tail if lead == 0 else jnp.concatenate([s[:, :lead], tail], axis=1)
            m_prev = m_ref[hh]
            m_new = jnp.maximum(m_prev, jnp.max(s, axis=1, keepdims=True))
            alpha = jnp.exp(m_prev - m_new)
            p = jnp.exp(s - jnp.tile(m_new, (1, width // LANES)))
            acc_ref[hh] = alpha * acc_ref[hh] + jnp.dot(p.astype(BF16), v_aug[hh],
                                                        preferred_element_type=F32)
            m_ref[hh] = m_new

    def body(j, carry):
        step(pl.multiple_of(j * tk, tk), tk, None)
        return carry

    q0 = qi * tq
    n_wide = q0 // tk
    lax.fori_loop(0, n_wide, body, 0)
    k_last = pl.multiple_of(n_wide * tk, tk)
    for m in range(tk // tq):
        @pl.when(q0 - n_wide * tk == m * tq)
        def _():
            step(k_last, (m + 1) * tq, m * tq)

    a0 = acc_ref[0]
    a1 = acc_ref[1]
    out = jnp.where(lane < dh, a0 / pltpu.roll(a0, dh, axis=1), a1 / pltpu.roll(a1, dh, axis=1))
    o_ref[0] = out.astype(o_ref.dtype)


def _fox_attn(q, k, v, n_heads, tq, tk):
    B, S, AW = q.shape
    dh = AW // n_heads
    assert 2 * dh == LANES and n_heads % 2 == 0 and tk % tq == 0 and tk % LANES == 0
    return pl.pallas_call(
        functools.partial(_fox_attn_kernel, tq, tk, dh),
        grid=(B, n_heads // 2, S // tq),
        in_specs=[pl.BlockSpec((1, tq, LANES), lambda b, p, i: (b, i, p)),
                  pl.BlockSpec((1, S, 2 * LANES), lambda b, p, i: (b, 0, p)),
                  pl.BlockSpec((1, S, LANES), lambda b, p, i: (b, 0, p))],
        out_specs=pl.BlockSpec((1, tq, LANES), lambda b, p, i: (b, i, p)),
        out_shape=jax.ShapeDtypeStruct((B, S, AW), BF16),
        scratch_shapes=[pltpu.VMEM((2, tq, LANES), F32), pltpu.VMEM((2, tq, LANES), F32)],
        compiler_params=_cparams("arbitrary", "arbitrary", "arbitrary"),
        name="fox_attn",
    )(q, k, v)


def _merge_kernel(tiles_per_seq, halo,
                  x_ref, a_ref, gb_ref, z_ref, zprev_ref, cw_ref, ga_ref, gv_ref,
                  wua_ref, wuc_ref, wo_ref, o_ref):
    i = pl.program_id(0)
    z = z_ref[...].astype(F32)
    tm = z.shape[0]
    prev = jnp.where(i % tiles_per_seq == 0, 0.0, zprev_ref[...].astype(F32))
    zext = jnp.concatenate([prev, z], axis=0)
    cw = cw_ref[...]
    kk = cw.shape[0]
    conv = z * cw[kk - 1:kk, :]
    for d in range(1, kk):
        conv = conv + pltpu.roll(zext, d, axis=0)[halo:halo + tm] * cw[kk - 1 - d:kk - d, :]
    c = (gb_ref[...].astype(F32) * conv).astype(BF16)
    ya = jnp.dot(a_ref[...], wua_ref[...], preferred_element_type=F32)
    yc = jnp.dot(c, wuc_ref[...], preferred_element_type=F32)
    merged = ga_ref[...].astype(F32) * ya + gv_ref[...].astype(F32) * yc
    o_ref[...] = x_ref[...] + jnp.dot(merged.astype(BF16), wo_ref[...],
                                      preferred_element_type=F32)


def _merge(x2, attn, gb, z, conv_w, ga, gv, w_branch, w_out, seq, tm):
    T, D = x2.shape
    aw = attn.shape[1]
    cw = z.shape[1]
    halo = 16
    assert conv_w.shape[0] - 1 <= halo
    wua = w_branch[0].astype(BF16)
    wuc = w_branch[1].astype(BF16)
    wo = w_out.astype(BF16)

    def full(a):
        return pl.BlockSpec(a.shape, lambda i: (0, 0))

    def tok(width):
        return pl.BlockSpec((tm, width), lambda i: (i, 0))

    hb = tm // halo
    return pl.pallas_call(
        functools.partial(_merge_kernel, seq // tm, halo),
        grid=(T // tm,),
        in_specs=[tok(D), tok(aw), tok(cw), tok(cw),
                  pl.BlockSpec((halo, cw), lambda i: (jnp.maximum(i * hb - 1, 0), 0)),
                  full(conv_w), tok(D), tok(D), full(wua), full(wuc), full(wo)],
        out_specs=tok(D),
        out_shape=jax.ShapeDtypeStruct((T, D), F32),
        compiler_params=_cparams("arbitrary"),
        name="merge",
    )(x2, attn, gb, z, z, conv_w, ga, gv, wua, wuc, wo)


def _top_rows(xs, k, want_rank=False):
    xs = list(xs)
    rows = [[] for _ in xs]
    for r in range(k):
        for i, x in enumerate(xs):
            m = jnp.max(x, axis=0, keepdims=True)
            rows[i].append(m)
            xs[i] = jnp.where(x == m, -REMOVED * (1.0 + r * 2.0 ** -23), x)
    if not want_rank:
        return rows
    ranks = [jnp.where(x <= -REMOVED, (x * (-1.0 / REMOVED) - 1.0) * 2.0 ** 23, float(k))
             for x in xs]
    return list(zip(rows, ranks))


def _ceil_to(n, m):
    return -(-n // m) * m


def _stack_rows(rows, t):
    n = len(rows)
    ridx = lax.broadcasted_iota(jnp.int32, (n, t), 0)
    out = jnp.zeros((n, t), F32)
    for r, row in enumerate(rows):
        out = jnp.where(ridx == r, row, out)
    return out


def _route(s1, s2, topk):
    t = s1.shape[1]
    (v1, rank1), (v2, rank2) = _top_rows([s1, s2], topk, True)
    v1a = _stack_rows(v1, t)
    v2a = _stack_rows(v2, t)
    few = 4
    cands = [v1[a] + v2a[:_ceil_to(topk // (a + 1), SUBLANES)]
             for a in range(topk) if topk // (a + 1) >= few]
    cands += [v2[b] + v1a[:_ceil_to(topk // (b + 1), SUBLANES)] for b in range(few - 1)]
    top, = _top_rows([jnp.concatenate(cands, axis=0)], topk)
    tau = top[topk - 1]
    zsum = jnp.zeros_like(tau)
    for r in range(topk):
        zsum = zsum + jnp.exp(top[r] - top[0])
    n_rows = []
    for a in range(topk):
        nb = _ceil_to(topk // (a + 1), SUBLANES)
        n_rows.append(jnp.sum(jnp.where(v1[a] + v2a[:nb] >= tau, 1.0, 0.0), axis=0,
                              keepdims=True))
    return rank1, rank2, v1[0], v2[0], zsum, n_rows


def _peer_prep_kernel(n_heads, half, topk,
                      x_ref, g_ref, wqt_ref, k1_ref, k2_ref, u_ref, v_ref,
                      ht_ref, rk_ref, e2_ref, c_ref, n_ref, ub_ref, vt_ref):
    ub_ref[...] = u_ref[...].astype(BF16)
    vt_ref[0] = v_ref[...].T.astype(BF16)
    h2 = _rms(x_ref[...], g_ref[...])
    ht = h2.T.astype(BF16)
    ht_ref[...] = ht
    qt = jnp.dot(wqt_ref[...], ht, preferred_element_type=F32)
    tp = ht.shape[1]
    for h in range(n_heads):
        q1 = qt[h * 2 * half:h * 2 * half + half].astype(BF16)
        q2 = qt[h * 2 * half + half:(h + 1) * 2 * half].astype(BF16)
        s1 = jnp.dot(k1_ref[h], q1, preferred_element_type=F32)
        s2 = jnp.dot(k2_ref[h], q2, preferred_element_type=F32)
        rank1, rank2, m1, m2, zsum, n_rows = _route(s1, s2, topk)
        rank1 = rank1.astype(BF16)
        n = jnp.zeros(s1.shape, BF16)
        for a in range(topk):
            n = jnp.where(rank1 == float(a), _row_to_bf16_tile(n_rows[a], s1.shape[0]), n)
        rank2 = rank2.astype(rk_ref.dtype)
        e2 = jnp.exp(s2 - m2).astype(e2_ref.dtype)
        for t in range(tp // TOK_CHUNK):
            rk_ref[h, t] = rank2[:, t * TOK_CHUNK:(t + 1) * TOK_CHUNK]
            e2_ref[h, t] = e2[:, t * TOK_CHUNK:(t + 1) * TOK_CHUNK]
        c_ref[h] = jnp.exp(s1 - m1) / zsum
        n_ref[h] = n.astype(F32)


def _peer_prep(x1, g_ffn, w_peer_q, k1, k2, u, v, tp):
    T, D = x1.shape
    _, PH, QD = w_peer_q.shape
    NK, HALF = k1.shape[1], k1.shape[2]
    NE = u.shape[0]
    assert NK >= PEER_TOPK and PEER_TOPK % 8 == 0
    steps = T // tp
    es = NE // steps
    eb = _peer_rows(NK) * NK
    assert NE % steps == 0 and es % LANES == 0 and eb % es == 0
    wqt = w_peer_q.reshape(D, PH * QD).T.astype(BF16)
    k1b = k1.astype(BF16)
    k2b = k2.astype(BF16)
    big = pl.BlockSpec((PH, NK, tp), lambda i: (0, 0, i))
    tiled = pl.BlockSpec((PH, tp // TOK_CHUNK, NK, TOK_CHUNK), lambda i: (0, i, 0, 0))
    rows = pl.BlockSpec((es, D), lambda i: (i, 0))
    return pl.pallas_call(
        functools.partial(_peer_prep_kernel, PH, HALF, PEER_TOPK),
        grid=(steps,),
        in_specs=[pl.BlockSpec((tp, D), lambda i: (i, 0)),
                  pl.BlockSpec(g_ffn.shape, lambda i: (0, 0)),
                  pl.BlockSpec(wqt.shape, lambda i: (0, 0)),
                  pl.BlockSpec(k1b.shape, lambda i: (0, 0, 0)),
                  pl.BlockSpec(k2b.shape, lambda i: (0, 0, 0)), rows, rows],
        out_specs=[pl.BlockSpec((D, tp), lambda i: (0, i)), tiled, tiled, big, big, rows,
                   pl.BlockSpec((1, D, es), lambda i: (i // (eb // es), 0, i % (eb // es)))],
        out_shape=[jax.ShapeDtypeStruct((D, T), BF16),
                   jax.ShapeDtypeStruct((PH, T // TOK_CHUNK, NK, TOK_CHUNK), BF16),
                   jax.ShapeDtypeStruct((PH, T // TOK_CHUNK, NK, TOK_CHUNK), BF16),
                   jax.ShapeDtypeStruct((PH, NK, T), F32),
                   jax.ShapeDtypeStruct((PH, NK, T), F32),
                   jax.ShapeDtypeStruct((NE, D), BF16),
                   jax.ShapeDtypeStruct((NE // eb, D, eb), BF16)],
        compiler_params=_cparams("arbitrary"),
        name="peer_prep",
    )(x1, g_ffn, wqt, k1b, k2b, u, v)


def _gelu(x):
    return 0.5 * x * (1.0 + lax.erf(x * (1.0 / math.sqrt(2.0))))


BF16_ROWS = 16


def _row_to_bf16_tile(row, n):
    packed = jnp.broadcast_to(row, (BF16_ROWS, row.shape[1])).astype(BF16)
    return jnp.tile(packed, (n // BF16_ROWS, 1))


def _peer_main_kernel(n_heads, nkeys, rows,
                      ht_ref, u_ref, vt_ref, rk_ref, e2_ref, c_ref, n_ref,
                      o_ref, acc_ref, w_ref):
    j = pl.program_id(1)
    tm = ht_ref.shape[1]

    @pl.when(j == 0)
    def _():
        acc_ref[...] = jnp.zeros_like(acc_ref)

    ht = ht_ref[...]
    for r in range(rows):
        act = jnp.dot(u_ref[r * nkeys:(r + 1) * nkeys, :], ht, preferred_element_type=F32)
        for ci in range(tm // TOK_CHUNK):
            tok = slice(ci * TOK_CHUNK, (ci + 1) * TOK_CHUNK)
            g = jnp.zeros((nkeys, TOK_CHUNK), BF16)
            for h in range(n_heads):
                n_b = _row_to_bf16_tile(n_ref[h, r:r + 1, tok], nkeys)
                c_b = _row_to_bf16_tile(c_ref[h, r:r + 1, tok], nkeys)
                g = g + jnp.where(rk_ref[h, ci] < n_b, e2_ref[h, ci] * c_b,
                                  jnp.zeros_like(g))
            w_ref[r * nkeys:(r + 1) * nkeys, tok] = g * _gelu(act[:, tok]).astype(BF16)
    acc_ref[...] += jnp.dot(vt_ref[0], w_ref[...], preferred_element_type=F32)

    @pl.when(j == pl.num_programs(1) - 1)
    def _():
        o_ref[...] = acc_ref[...].T.astype(o_ref.dtype)


def _peer_rows(nkeys):
    return min(32, nkeys)


def _peer_main(ht, ub, vt, rk, e2, c, n, tm):
    D, T = ht.shape
    PH, _, NK, _ = rk.shape
    NE = ub.shape[0]
    rows = _peer_rows(NK)
    eb = rows * NK
    assert vt.shape == (NE // eb, D, eb)
    tile = pl.BlockSpec((PH, tm // TOK_CHUNK, NK, TOK_CHUNK), lambda i, j: (0, i, 0, 0))
    rowblk = pl.BlockSpec((PH, rows, tm), lambda i, j: (0, j, i))
    return pl.pallas_call(
        functools.partial(_peer_main_kernel, PH, NK, rows),
        grid=(T // tm, NE // eb),
        in_specs=[pl.BlockSpec((D, tm), lambda i, j: (0, i)),
                  pl.BlockSpec((eb, D), lambda i, j: (j, 0)),
                  pl.BlockSpec((1, D, eb), lambda i, j: (j, 0, 0)),
                  tile, tile, rowblk, rowblk],
        out_specs=pl.BlockSpec((tm, D), lambda i, j: (i, 0)),
        out_shape=jax.ShapeDtypeStruct((T, D), BF16),
        scratch_shapes=[pltpu.VMEM((D, tm), F32), pltpu.VMEM((eb, tm), BF16)],
        compiler_params=_cparams("arbitrary", "arbitrary"),
        name="peer_main",
    )(ht, ub, vt, rk, e2, c, n)


def _ple_out_kernel(final_norm, x_ref, f_ref, p_ref, gp_ref, wg_ref, wp_ref, gf_ref, o_ref):
    x = x_ref[...] + f_ref[...].astype(F32)
    h = _rms(x, gp_ref[...]).astype(BF16)
    gate = jax.nn.sigmoid(jnp.dot(h, wg_ref[...], preferred_element_type=F32))
    proj = jnp.dot(p_ref[...].astype(BF16), wp_ref[...], preferred_element_type=F32)
    y = x + gate * proj
    o_ref[...] = _rms(y, gf_ref[...]) if final_norm else y


def _ple_out(x1, ffn, p2, g_ple, w_gate, w_proj, g_final, final_norm, tm):
    T, D = x1.shape
    wg = w_gate.astype(BF16)
    wp = w_proj.astype(BF16)

    def full(a):
        return pl.BlockSpec(a.shape, lambda i: (0, 0))

    return pl.pallas_call(
        functools.partial(_ple_out_kernel, final_norm),
        grid=(T // tm,),
        in_specs=[pl.BlockSpec((tm, D), lambda i: (i, 0)),
                  pl.BlockSpec((tm, D), lambda i: (i, 0)),
                  pl.BlockSpec((tm, p2.shape[1]), lambda i: (i, 0)),
                  full(g_ple), full(wg), full(wp), full(g_final)],
        out_specs=pl.BlockSpec((tm, D), lambda i: (i, 0)),
        out_shape=jax.ShapeDtypeStruct((T, D), F32),
        compiler_params=_cparams("arbitrary"),
        name="ple_out",
    )(x1, ffn, p2, g_ple, wg, wp, g_final)


def _tile(n, want):
    t = min(n, want)
    assert n % t == 0
    return t


def kernel(x, p, g_mix, w_in, b_f, conv_w, w_branch, w_out, g_ffn, w_peer_q, peer_k1,
           peer_k2, peer_u, peer_v, g_ple, w_ple_gate, w_ple_proj, g_final):
    B, S, D = x.shape
    T = B * S
    depth = w_in.shape[0]
    n_heads = b_f.shape[1]
    aw = w_branch.shape[2]
    cw = conv_w.shape[2]
    tm = _tile(S, 512)
    xt = x.reshape(T, D)
    for i in range(depth):
        q, k, v, gb, z, ga, gv = _in_proj(
            xt, g_mix[i].reshape(1, D), w_in[i], b_f[i], (aw, n_heads, cw), S, tm)
        attn = _fox_attn(q.reshape(B, S, aw), k.reshape(B, S, -1), v.reshape(B, S, aw),
                         n_heads, _tile(S, 512), _tile(S, 2048))
        x1 = _merge(xt, attn.reshape(T, aw), gb, z, conv_w[i], ga, gv, w_branch[i], w_out[i],
                    S, _tile(S, 1024))
        ht, rk, e2, cc, nn, ub, vt = _peer_prep(x1, g_ffn[i].reshape(1, D), w_peer_q[i],
                                                peer_k1[i], peer_k2[i], peer_u[i], peer_v[i],
                                                _tile(T, 256))
        ffn = _peer_main(ht, ub, vt, rk, e2, cc, nn, _tile(T, 512))
        xt = _ple_out(x1, ffn, p[i].reshape(T, p.shape[-1]), g_ple[i].reshape(1, D),
                      w_ple_gate[i], w_ple_proj[i], g_final.reshape(1, D), i == depth - 1,
                      _tile(S, 1024))
    return xt.reshape(B, S, D)
```

```python
import functools
import math

import jax
import jax.numpy as jnp
import numpy as np
from jax import lax
from jax.experimental import pallas as pl
from jax.experimental.pallas import tpu as pltpu

EPS = 1e-6
PEER_TOPK = 16
LANES = 128
SUBLANES = 8
NEG_BIG = -1e30
REMOVED = 2.0 ** 100
TOK_CHUNK = 256
VMEM_LIMIT = 56 * 1024 * 1024

F32 = jnp.float32
BF16 = jnp.bfloat16


def _cparams(*sem, flags=None):
    return pltpu.CompilerParams(dimension_semantics=sem, vmem_limit_bytes=VMEM_LIMIT,
                                flags=flags)


def _rms(x, g):
    return x * lax.rsqrt(jnp.mean(x * x, axis=-1, keepdims=True) + EPS) * g


def _log_sigmoid(x):
    return jnp.minimum(x, 0.0) - jnp.log1p(jnp.exp(-jnp.abs(x)))


def _split3(x):
    hi = x.astype(BF16)
    r = x - hi.astype(F32)
    mid = r.astype(BF16)
    lo = (r - mid.astype(F32)).astype(BF16)
    return hi, mid, lo


def _in_proj_kernel(tiles_per_seq, n_heads, scale,
                    x_ref, g_ref, wq_ref, wk_ref, wv_ref, wf_ref, bf_ref, wgb_ref, wgc_ref,
                    wxc_ref, wga_ref, wgv_ref, tri_ref, place_ref,
                    q_ref, k_ref, v_ref, gb_ref, z_ref, ga_ref, gv_ref, carry_ref):
    i = pl.program_id(0)
    h = _rms(x_ref[...], g_ref[...]).astype(BF16)

    def mm(w_ref):
        return jnp.dot(h, w_ref[...], preferred_element_type=F32)

    q_ref[...] = (mm(wq_ref) * scale).astype(BF16)
    v_ref[...] = mm(wv_ref).astype(BF16)
    gb_ref[...] = mm(wgb_ref).astype(BF16)
    z_ref[...] = (mm(wgc_ref) * mm(wxc_ref)).astype(BF16)
    ga_ref[...] = jax.nn.sigmoid(mm(wga_ref)).astype(BF16)
    gv_ref[...] = jax.nn.sigmoid(mm(wgv_ref)).astype(BF16)

    logf = _log_sigmoid(mm(wf_ref) + bf_ref[...])
    lane = lax.broadcasted_iota(jnp.int32, logf.shape, 1)
    logf = jnp.where(lane < n_heads, logf, 0.0)
    tri = tri_ref[...]
    hi, mid, lo = _split3(logf)
    cs = (jnp.dot(tri, hi, preferred_element_type=F32)
          + jnp.dot(tri, mid, preferred_element_type=F32)
          + jnp.dot(tri, lo, preferred_element_type=F32))

    @pl.when(i % tiles_per_seq == 0)
    def _():
        carry_ref[...] = jnp.zeros_like(carry_ref)

    c = cs + carry_ref[...]
    tm = c.shape[0]
    carry_ref[...] = c[tm - 1:tm, :]
    pieces = jnp.concatenate(_split3(-c), axis=1)
    feats = mm(wk_ref).astype(BF16)
    bias = jnp.dot(pieces, place_ref[...], preferred_element_type=F32).astype(BF16)
    for pair in range(n_heads // 2):
        src = slice(pair * LANES, (pair + 1) * LANES)
        k_ref[:, 2 * pair * LANES:(2 * pair + 1) * LANES] = feats[:, src]
        k_ref[:, (2 * pair + 1) * LANES:(2 * pair + 2) * LANES] = bias[:, src]


def _in_proj(x2, g_mix, w_in, b_f, sizes, seq, tm):
    T, D = x2.shape
    aw, n_heads, cw = sizes
    offs = [0, aw, 2 * aw, 3 * aw, 3 * aw + n_heads, 3 * aw + n_heads + cw,
            3 * aw + n_heads + 2 * cw, 3 * aw + n_heads + 3 * cw,
            3 * aw + n_heads + 3 * cw + D, 3 * aw + n_heads + 3 * cw + 2 * D]
    seg = [w_in[:, offs[n]:offs[n + 1]] for n in range(9)]
    wq, wk, wv, wf, wgb, wgc, wxc, wga, wgv = seg
    wf = jnp.pad(wf, ((0, 0), (0, LANES - n_heads)))
    bf = jnp.pad(b_f.reshape(1, n_heads), ((0, 0), (0, LANES - n_heads)))
    dh = aw // n_heads
    assert 2 * dh == LANES
    kw = n_heads * LANES
    place = np.zeros((3 * LANES, kw // 2), np.float32)
    for hd in range(n_heads):
        for s in range(3):
            place[s * LANES + hd, (hd // 2) * LANES + 3 * (hd % 2) + s] = 1.0
    place = jnp.asarray(place, BF16)
    ws = [w.astype(BF16) for w in (wq, wk, wv, wf)] + [bf] + \
         [w.astype(BF16) for w in (wgb, wgc, wxc, wga, wgv)]
    tri = (lax.broadcasted_iota(jnp.int32, (tm, tm), 0)
           >= lax.broadcasted_iota(jnp.int32, (tm, tm), 1)).astype(BF16)
    scale = dh ** -0.5

    def full(a):
        return pl.BlockSpec(a.shape, lambda i: (0, 0))

    def tok(width):
        return pl.BlockSpec((tm, width), lambda i: (i, 0))

    outs = [(aw, BF16), (kw, BF16), (aw, BF16), (cw, BF16), (cw, BF16), (D, BF16), (D, BF16)]
    return pl.pallas_call(
        functools.partial(_in_proj_kernel, seq // tm, n_heads, scale),
        grid=(T // tm,),
        in_specs=[tok(D), full(g_mix)] + [full(w) for w in ws] + [full(tri), full(place)],
        out_specs=[tok(w) for w, _ in outs],
        out_shape=[jax.ShapeDtypeStruct((T, w), dt) for w, dt in outs],
        scratch_shapes=[pltpu.VMEM((1, LANES), F32)],
        compiler_params=_cparams("arbitrary"),
        name="in_proj",
    )(x2, g_mix, *ws, tri, place)


def _fox_attn_kernel(tq, tk, dh, q_ref, k_ref, v_ref, o_ref, m_ref, acc_ref):
    qi = pl.program_id(2)
    qp = q_ref[0]
    lane = lax.broadcasted_iota(jnp.int32, qp.shape, 1)
    zero = jnp.zeros_like(qp)
    q_aug = []
    for hh in range(2):
        feat = jnp.where(lane < dh, qp, zero) if hh == 0 else jnp.where(lane >= dh, qp, zero)
        pick = jnp.where(lane < 3 * hh, 0.0, jnp.where(lane < 3 * hh + 3, 1.0, 0.0)).astype(BF16)
        q_aug.append(jnp.concatenate([feat, pick], axis=1))

    m_ref[...] = jnp.full(m_ref.shape, NEG_BIG, F32)
    acc_ref[...] = jnp.zeros(acc_ref.shape, F32)

    def step(k0, width, lead):
        kb = k_ref[0, pl.ds(k0, width), :]
        vb = v_ref[0, pl.ds(k0, width), :]
        vlane = lax.broadcasted_iota(jnp.int32, vb.shape, 1)
        vone = jnp.ones_like(vb)
        v_aug = (jnp.where(vlane < dh, vb, vone), jnp.where(vlane >= dh, vb, vone))
        for hh in range(2):
            s = lax.dot_general(q_aug[hh], kb, (((1,), (1,)), ((), ())),
                                preferred_element_type=F32)
            if lead is not None:
                row = lax.broadcasted_iota(jnp.int32, (tq, width - lead), 0)
                col = lax.broadcasted_iota(jnp.int32, (tq, width - lead), 1)
                tail = jnp.where(col <= row, s[:, lead:], NEG_BIG)
                s = tail if lead == 0 else jnp.concatenate([s[:, :lead], tail], axis=1)
            m_prev = m_ref[hh]
            m_new = jnp.maximum(m_prev, jnp.max(s, axis=1, keepdims=True))
            alpha = jnp.exp(m_prev - m_new)
            p = jnp.exp(s - jnp.tile(m_new, (1, width // LANES)))
            acc_ref[hh] = alpha * acc_ref[hh] + jnp.dot(p.astype(BF16), v_aug[hh],
                                                        preferred_element_type=F32)
            m_ref[hh] = m_new

    def body(j, carry):
        step(pl.multiple_of(j * tk, tk), tk, None)
        return carry

    q0 = qi * tq
    n_wide = q0 // tk
    lax.fori_loop(0, n_wide, body, 0)
    k_last = pl.multiple_of(n_wide * tk, tk)
    for m in range(tk // tq):
        @pl.when(q0 - n_wide * tk == m * tq)
        def _():
            step(k_last, (m + 1) * tq, m * tq)

    a0 = acc_ref[0]
    a1 = acc_ref[1]
    out = jnp.where(lane < dh, a0 / pltpu.roll(a0, dh, axis=1), a1 / pltpu.roll(a1, dh, axis=1))
    o_ref[0] = out.astype(o_ref.dtype)


def _fox_attn(q, k, v, n_heads, tq, tk):
    B, S, AW = q.shape
    dh = AW // n_heads
    assert 2 * dh == LANES and n_heads % 2 == 0 and tk % tq == 0 and tk % LANES == 0
    return pl.pallas_call(
        functools.partial(_fox_attn_kernel, tq, tk, dh),
        grid=(B, n_heads // 2, S // tq),
        in_specs=[pl.BlockSpec((1, tq, LANES), lambda b, p, i: (b, i, p)),
                  pl.BlockSpec((1, S, 2 * LANES), lambda b, p, i: (b, 0, p)),
                  pl.BlockSpec((1, S, LANES), lambda b, p, i: (b, 0, p))],
        out_specs=pl.BlockSpec((1, tq, LANES), lambda b, p, i: (b, i, p)),
        out_shape=jax.ShapeDtypeStruct((B, S, AW), BF16),
        scratch_shapes=[pltpu.VMEM((2, tq, LANES), F32), pltpu.VMEM((2, tq, LANES), F32)],
        compiler_params=_cparams("arbitrary", "arbitrary", "arbitrary"),
        name="fox_attn",
    )(q, k, v)


def _merge_kernel(tiles_per_seq, halo,
                  x_ref, a_ref, gb_ref, z_ref, zprev_ref, cw_ref, ga_ref, gv_ref,
                  wua_ref, wuc_ref, wo_ref, o_ref):
    i = pl.program_id(0)
    z = z_ref[...].astype(F32)
    tm = z.shape[0]
    prev = jnp.where(i % tiles_per_seq == 0, 0.0, zprev_ref[...].astype(F32))
    zext = jnp.concatenate([prev, z], axis=0)
    cw = cw_ref[...]
    kk = cw.shape[0]
    conv = z * cw[kk - 1:kk, :]
    for d in range(1, kk):
        conv = conv + pltpu.roll(zext, d, axis=0)[halo:halo + tm] * cw[kk - 1 - d:kk - d, :]
    c = (gb_ref[...].astype(F32) * conv).astype(BF16)
    ya = jnp.dot(a_ref[...], wua_ref[...], preferred_element_type=F32)
    yc = jnp.dot(c, wuc_ref[...], preferred_element_type=F32)
    merged = ga_ref[...].astype(F32) * ya + gv_ref[...].astype(F32) * yc
    o_ref[...] = x_ref[...] + jnp.dot(merged.astype(BF16), wo_ref[...],
                                      preferred_element_type=F32)


def _merge(x2, attn, gb, z, conv_w, ga, gv, w_branch, w_out, seq, tm):
    T, D = x2.shape
    aw = attn.shape[1]
    cw = z.shape[1]
    halo = 16
    assert conv_w.shape[0] - 1 <= halo
    wua = w_branch[0].astype(BF16)
    wuc = w_branch[1].astype(BF16)
    wo = w_out.astype(BF16)

    def full(a):
        return pl.BlockSpec(a.shape, lambda i: (0, 0))

    def tok(width):
        return pl.BlockSpec((tm, width), lambda i: (i, 0))

    hb = tm // halo
    return pl.pallas_call(
        functools.partial(_merge_kernel, seq // tm, halo),
        grid=(T // tm,),
        in_specs=[tok(D), tok(aw), tok(cw), tok(cw),
                  pl.BlockSpec((halo, cw), lambda i: (jnp.maximum(i * hb - 1, 0), 0)),
                  full(conv_w), tok(D), tok(D), full(wua), full(wuc), full(wo)],
        out_specs=tok(D),
        out_shape=jax.ShapeDtypeStruct((T, D), F32),
        compiler_params=_cparams("arbitrary"),
        name="merge",
    )(x2, attn, gb, z, z, conv_w, ga, gv, wua, wuc, wo)


def _top_rows(xs, k, want_rank=False):
    xs = list(xs)
    rows = [[] for _ in xs]
    for r in range(k):
        for i, x in enumerate(xs):
            m = jnp.max(x, axis=0, keepdims=True)
            rows[i].append(m)
            xs[i] = jnp.where(x == m, -REMOVED * (1.0 + r * 2.0 ** -23), x)
    if not want_rank:
        return rows
    ranks = [jnp.where(x <= -REMOVED, (x * (-1.0 / REMOVED) - 1.0) * 2.0 ** 23, float(k))
             for x in xs]
    return list(zip(rows, ranks))


def _ceil_to(n, m):
    return -(-n // m) * m


def _stack_rows(rows, t):
    n = len(rows)
    ridx = lax.broadcasted_iota(jnp.int32, (n, t), 0)
    out = jnp.zeros((n, t), F32)
    for r, row in enumerate(rows):
        out = jnp.where(ridx == r, row, out)
    return out


def _route(s1, s2, topk):
    t = s1.shape[1]
    (v1, rank1), (v2, rank2) = _top_rows([s1, s2], topk, True)
    v1a = _stack_rows(v1, t)
    v2a = _stack_rows(v2, t)
    few = 4
    cands = [v1[a] + v2a[:_ceil_to(topk // (a + 1), SUBLANES)]
             for a in range(topk) if topk // (a + 1) >= few]
    cands += [v2[b] + v1a[:_ceil_to(topk // (b + 1), SUBLANES)] for b in range(few - 1)]
    top, = _top_rows([jnp.concatenate(cands, axis=0)], topk)
    tau = top[topk - 1]
    zsum = jnp.zeros_like(tau)
    for r in range(topk):
        zsum = zsum + jnp.exp(top[r] - top[0])
    n_rows = []
    for a in range(topk):
        nb = _ceil_to(topk // (a + 1), SUBLANES)
        n_rows.append(jnp.sum(jnp.where(v1[a] + v2a[:nb] >= tau, 1.0, 0.0), axis=0,
                              keepdims=True))
    return rank1, rank2, v1[0], v2[0], zsum, n_rows


def _peer_prep_kernel(n_heads, half, topk,
                      x_ref, g_ref, wqt_ref, k1_ref, k2_ref, u_ref, v_ref,
                      ht_ref, rk_ref, e2_ref, c_ref, n_ref, ub_ref, vt_ref):
    ub_ref[...] = u_ref[...].astype(BF16)
    vt_ref[...] = v_ref[...].T.astype(BF16)
    h2 = _rms(x_ref[...], g_ref[...])
    ht_ref[...] = h2.T.astype(BF16)
    q = jnp.dot(h2.astype(BF16), wqt_ref[...], preferred_element_type=F32)
    tp = q.shape[0]
    for h in range(n_heads):
        q1 = q[:, h * 2 * half:h * 2 * half + half].T.astype(BF16)
        q2 = q[:, h * 2 * half + half:(h + 1) * 2 * half].T.astype(BF16)
        s1 = jnp.dot(k1_ref[h], q1, preferred_element_type=F32)
        s2 = jnp.dot(k2_ref[h], q2, preferred_element_type=F32)
        rank1, rank2, m1, m2, zsum, n_rows = _route(s1, s2, topk)
        rank1 = rank1.astype(BF16)
        n = jnp.zeros(s1.shape, BF16)
        for a in range(topk):
            n = jnp.where(rank1 == float(a), _row_to_bf16_tile(n_rows[a], s1.shape[0]), n)
        rank2 = rank2.astype(rk_ref.dtype)
        e2 = jnp.exp(s2 - m2).astype(e2_ref.dtype)
        for t in range(tp // TOK_CHUNK):
            rk_ref[h, t] = rank2[:, t * TOK_CHUNK:(t + 1) * TOK_CHUNK]
            e2_ref[h, t] = e2[:, t * TOK_CHUNK:(t + 1) * TOK_CHUNK]
        c_ref[h] = jnp.exp(s1 - m1) / zsum
        n_ref[h] = n.astype(F32)


def _peer_prep(x1, g_ffn, w_peer_q, k1, k2, u, v, tp):
    T, D = x1.shape
    _, PH, QD = w_peer_q.shape
    NK, HALF = k1.shape[1], k1.shape[2]
    NE = u.shape[0]
    assert NK >= PEER_TOPK and PEER_TOPK % 8 == 0
    steps = T // tp
    es = NE // steps
    assert NE % steps == 0 and es % LANES == 0
    wqt = w_peer_q.reshape(D, PH * QD).astype(BF16)
    k1b = k1.astype(BF16)
    k2b = k2.astype(BF16)
    big = pl.BlockSpec((PH, NK, tp), lambda i: (0, 0, i))
    tiled = pl.BlockSpec((PH, tp // TOK_CHUNK, NK, TOK_CHUNK), lambda i: (0, i, 0, 0))
    rows = pl.BlockSpec((es, D), lambda i: (i, 0))
    return pl.pallas_call(
        functools.partial(_peer_prep_kernel, PH, HALF, PEER_TOPK),
        grid=(steps,),
        in_specs=[pl.BlockSpec((tp, D), lambda i: (i, 0)),
                  pl.BlockSpec(g_ffn.shape, lambda i: (0, 0)),
                  pl.BlockSpec(wqt.shape, lambda i: (0, 0)),
                  pl.BlockSpec(k1b.shape, lambda i: (0, 0, 0)),
                  pl.BlockSpec(k2b.shape, lambda i: (0, 0, 0)), rows, rows],
        out_specs=[pl.BlockSpec((D, tp), lambda i: (0, i)), tiled, tiled, big, big, rows,
                   pl.BlockSpec((D, es), lambda i: (0, i))],
        out_shape=[jax.ShapeDtypeStruct((D, T), BF16),
                   jax.ShapeDtypeStruct((PH, T // TOK_CHUNK, NK, TOK_CHUNK), BF16),
                   jax.ShapeDtypeStruct((PH, T // TOK_CHUNK, NK, TOK_CHUNK), BF16),
                   jax.ShapeDtypeStruct((PH, NK, T), F32),
                   jax.ShapeDtypeStruct((PH, NK, T), F32),
                   jax.ShapeDtypeStruct((NE, D), BF16),
                   jax.ShapeDtypeStruct((D, NE), BF16)],
        compiler_params=_cparams("arbitrary"),
        name="peer_prep",
    )(x1, g_ffn, wqt, k1b, k2b, u, v)


def _gelu(x):
    return 0.5 * x * (1.0 + lax.erf(x * (1.0 / math.sqrt(2.0))))


BF16_ROWS = 16


def _row_to_bf16_tile(row, n):
    packed = jnp.broadcast_to(row, (BF16_ROWS, row.shape[1])).astype(BF16)
    return jnp.tile(packed, (n // BF16_ROWS, 1))


def _peer_main_kernel(n_heads, nkeys, rows,
                      ht_ref, u_ref, vt_ref, rk_ref, e2_ref, c_ref, n_ref,
                      o_ref, acc_ref, w_ref):
    j = pl.program_id(1)
    tm = ht_ref.shape[1]

    @pl.when(j == 0)
    def _():
        acc_ref[...] = jnp.zeros_like(acc_ref)

    ht = ht_ref[...]
    for r in range(rows):
        act = jnp.dot(u_ref[r * nkeys:(r + 1) * nkeys, :], ht, preferred_element_type=F32)
        for ci in range(tm // TOK_CHUNK):
            tok = slice(ci * TOK_CHUNK, (ci + 1) * TOK_CHUNK)
            g = jnp.zeros((nkeys, TOK_CHUNK), BF16)
            for h in range(n_heads):
                n_b = _row_to_bf16_tile(n_ref[h, r:r + 1, tok], nkeys)
                c_b = _row_to_bf16_tile(c_ref[h, r:r + 1, tok], nkeys)
                g = g + jnp.where(rk_ref[h, ci] < n_b, e2_ref[h, ci] * c_b,
                                  jnp.zeros_like(g))
            w_ref[r * nkeys:(r + 1) * nkeys, tok] = g * _gelu(act[:, tok]).astype(BF16)
    acc_ref[...] += jnp.dot(vt_ref[...], w_ref[...], preferred_element_type=F32)

    @pl.when(j == pl.num_programs(1) - 1)
    def _():
        o_ref[...] = acc_ref[...].T.astype(o_ref.dtype)


def _peer_main(ht, ub, vt, rk, e2, c, n, tm):
    D, T = ht.shape
    PH, _, NK, _ = rk.shape
    NE = ub.shape[0]
    rows = min(32, NK)
    eb = rows * NK
    tile = pl.BlockSpec((PH, tm // TOK_CHUNK, NK, TOK_CHUNK), lambda i, j: (0, i, 0, 0))
    rowblk = pl.BlockSpec((PH, rows, tm), lambda i, j: (0, j, i))
    return pl.pallas_call(
        functools.partial(_peer_main_kernel, PH, NK, rows),
        grid=(T // tm, NE // eb),
        in_specs=[pl.BlockSpec((D, tm), lambda i, j: (0, i)),
                  pl.BlockSpec((eb, D), lambda i, j: (j, 0)),
                  pl.BlockSpec((D, eb), lambda i, j: (0, j)),
                  tile, tile, rowblk, rowblk],
        out_specs=pl.BlockSpec((tm, D), lambda i, j: (i, 0)),
        out_shape=jax.ShapeDtypeStruct((T, D), BF16),
        scratch_shapes=[pltpu.VMEM((D, tm), F32), pltpu.VMEM((eb, tm), BF16)],
        compiler_params=_cparams("arbitrary", "arbitrary"),
        name="peer_main",
    )(ht, ub, vt, rk, e2, c, n)


def _ple_out_kernel(final_norm, x_ref, f_ref, p_ref, gp_ref, wg_ref, wp_ref, gf_ref, o_ref):
    x = x_ref[...] + f_ref[...].astype(F32)
    h = _rms(x, gp_ref[...]).astype(BF16)
    gate = jax.nn.sigmoid(jnp.dot(h, wg_ref[...], preferred_element_type=F32))
    proj = jnp.dot(p_ref[...].astype(BF16), wp_ref[...], preferred_element_type=F32)
    y = x + gate * proj
    o_ref[...] = _rms(y, gf_ref[...]) if final_norm else y


def _ple_out(x1, ffn, p2, g_ple, w_gate, w_proj, g_final, final_norm, tm):
    T, D = x1.shape
    wg = w_gate.astype(BF16)
    wp = w_proj.astype(BF16)

    def full(a):
        return pl.BlockSpec(a.shape, lambda i: (0, 0))

    return pl.pallas_call(
        functools.partial(_ple_out_kernel, final_norm),
        grid=(T // tm,),
        in_specs=[pl.BlockSpec((tm, D), lambda i: (i, 0)),
                  pl.BlockSpec((tm, D), lambda i: (i, 0)),
                  pl.BlockSpec((tm, p2.shape[1]), lambda i: (i, 0)),
                  full(g_ple), full(wg), full(wp), full(g_final)],
        out_specs=pl.BlockSpec((tm, D), lambda i: (i, 0)),
        out_shape=jax.ShapeDtypeStruct((T, D), F32),
        compiler_params=_cparams("arbitrary"),
        name="ple_out",
    )(x1, ffn, p2, g_ple, wg, wp, g_final)


def _tile(n, want):
    t = min(n, want)
    assert n % t == 0
    return t


def kernel(x, p, g_mix, w_in, b_f, conv_w, w_branch, w_out, g_ffn, w_peer_q, peer_k1,
           peer_k2, peer_u, peer_v, g_ple, w_ple_gate, w_ple_proj, g_final):
    B, S, D = x.shape
    T = B * S
    depth = w_in.shape[0]
    n_heads = b_f.shape[1]
    aw = w_branch.shape[2]
    cw = conv_w.shape[2]
    tm = _tile(S, 512)
    xt = x.reshape(T, D)
    for i in range(depth):
        q, k, v, gb, z, ga, gv = _in_proj(
            xt, g_mix[i].reshape(1, D), w_in[i], b_f[i], (aw, n_heads, cw), S, tm)
        attn = _fox_attn(q.reshape(B, S, aw), k.reshape(B, S, -1), v.reshape(B, S, aw),
                         n_heads, _tile(S, 512), _tile(S, 2048))
        x1 = _merge(xt, attn.reshape(T, aw), gb, z, conv_w[i], ga, gv, w_branch[i], w_out[i],
                    S, _tile(S, 1024))
        ht, rk, e2, cc, nn, ub, vt = _peer_prep(x1, g_ffn[i].reshape(1, D), w_peer_q[i],
                                                peer_k1[i], peer_k2[i], peer_u[i], peer_v[i],
                                                _tile(T, 256))
        ffn = _peer_main(ht, ub, vt, rk, e2, cc, nn, _tile(T, 512))
        xt = _ple_out(x1, ffn, p[i].reshape(T, p.shape[-1]), g_ple[i].reshape(1, D),
                      w_ple_gate[i], w_ple_proj[i], g_final.reshape(1, D), i == depth - 1,
                      _tile(S, 1024))
    return xt.reshape(B, S, D)
```
